```python
import math
import jax, jax.numpy as jnp
from jax import lax
import numpy as np

D_MODEL = 2048
BATCH = 4
SEQ = 2048
DEPTH = 2
DEC_BATCH = 32
DEC_SEQ = 8
PAST_LEN = 16384
PAGE_SIZE = 128

D_INNER = D_MODEL
SSD_HEAD_DIM = 64
SSD_HEADS = D_INNER // SSD_HEAD_DIM
SSD_GROUPS = 4
D_STATE = 128
CONV_W = 4
CONV_DIM = D_INNER + 2 * SSD_GROUPS * D_STATE
SSD_CHUNK = 128
SWA_HEAD_DIM = 64
SWA_HEADS = D_MODEL // SWA_HEAD_DIM
SWA_KV_HEADS = SWA_HEADS // 8
WINDOW = 128
N_MEM = 256
MEM_HEADS = 4
MEM_HEAD_DIM = D_MODEL // MEM_HEADS
PEER_HEADS = 8
N_KEYS = 128
N_EXPERTS = N_KEYS * N_KEYS
PEER_QUERY_DIM = 256
PEER_HALF = PEER_QUERY_DIM // 2
PEER_TOPK = 16
PEER_TOKEN_BLOCK = 128
N_BRANCH = 3
EPS = 1e-6

_SPLITS = (D_INNER, CONV_DIM, SSD_HEADS, SWA_HEADS * SWA_HEAD_DIM, SWA_KV_HEADS * SWA_HEAD_DIM,
           SWA_KV_HEADS * SWA_HEAD_DIM, MEM_HEADS * MEM_HEAD_DIM, N_BRANCH * D_MODEL)
IN_DIM = sum(_SPLITS)

kernel_name = 'hybrid_ssd_swa_mem_peer_step'

F32 = jnp.float32


def _split_columns(y):
    parts, start = [], 0
    for w in _SPLITS:
        parts.append(y[..., start:start + w])
        start += w
    return parts


def _rmsnorm(x, g):
    xf = x.astype(F32)
    xf = xf * lax.rsqrt(jnp.mean(xf * xf, axis=-1, keepdims=True) + EPS)
    return (xf * g.astype(F32)).astype(x.dtype)


def _alibi_slopes(n):
    return jnp.exp2(-8.0 * jnp.arange(1, n + 1, dtype=F32) / n)


def _causal_dwconv(xbc, buf, w, b):
    xpad = jnp.concatenate([buf.astype(xbc.dtype), xbc], axis=1)
    y = lax.conv_general_dilated(xpad, w[:, None, :].astype(xbc.dtype), (1,), 'VALID',
                                 dimension_numbers=('NWC', 'WIO', 'NWC'),
                                 feature_group_count=xbc.shape[-1])
    return y + b.astype(xbc.dtype), xpad[:, -(CONV_W - 1):]


def _ssd_scan(xs, dt, A, bm, cm, state0):
    bsz, L = xs.shape[0], xs.shape[1]
    hg = SSD_HEADS // SSD_GROUPS
    q = min(SSD_CHUNK, L)
    pad = (-L) % q
    nc = (L + pad) // q

    def chunks(t):
        t = t.astype(F32)
        t = jnp.pad(t, [(0, 0), (0, pad)] + [(0, 0)] * (t.ndim - 2))
        return jnp.moveaxis(t.reshape((bsz, nc, q) + t.shape[2:]), 1, 0)

    xc = chunks(xs.reshape(bsz, L, SSD_GROUPS, hg, SSD_HEAD_DIM))
    dc = chunks(dt.reshape(bsz, L, SSD_GROUPS, hg))
    bc = chunks(bm)
    cc = chunks(cm)
    a_g = A.astype(F32).reshape(SSD_GROUPS, hg)
    causal = jnp.tril(jnp.ones((q, q), bool))[None, :, :, None, None]

    def step(S, inp):
        x, d, b, c = inp
        cum = jnp.cumsum(d * a_g, axis=1)
        seg = cum[:, :, None] - cum[:, None, :]
        decay = jnp.exp(jnp.where(causal, seg, -jnp.inf))
        cb = jnp.einsum('bign,bjgn->bijg', c, b)
        y = jnp.einsum('bijg,bijgh,bjgh,bjghp->bighp', cb, decay, d, x)
        y = y + jnp.einsum('bign,bghpn->bighp', c, S) * jnp.exp(cum)[..., None]
        w_end = jnp.exp(cum[:, -1:] - cum) * d
        S = S * jnp.exp(cum[:, -1])[..., None, None] + jnp.einsum('bjgh,bjghp,bjgn->bghpn', w_end, x, b)
        return S, y

    S0 = state0.astype(F32).reshape(bsz, SSD_GROUPS, hg, SSD_HEAD_DIM, D_STATE)
    S, ys = lax.scan(step, S0, (xc, dc, bc, cc))
    ys = jnp.moveaxis(ys, 0, 1).reshape(bsz, nc * q, SSD_HEADS, SSD_HEAD_DIM)[:, :L]
    return ys, S.reshape(bsz, SSD_HEADS, SSD_HEAD_DIM, D_STATE)


def _window_attention(qb, kb, vb, key_valid, sinks, slopes):
    lq, lk = qb.shape[2], kb.shape[2]
    dist = jnp.arange(lq)[:, None] + WINDOW - jnp.arange(lk)[None, :]
    mask = ((dist >= 0) & (dist <= WINDOW))[None] & key_valid[:, None, :]
    s = jnp.einsum('bnqhgd,bnkhd->bnhgqk', qb, kb).astype(F32) * (SWA_HEAD_DIM ** -0.5)
    s = s - slopes[:, :, None, None] * dist.astype(F32)
    s = jnp.where(mask[None, :, None, None], s, -jnp.inf)
    sink = sinks[:, :, None, None]
    m = jnp.maximum(jnp.max(s, axis=-1, keepdims=True), sink)
    p = jnp.exp(s - m)
    attn = p / (jnp.sum(p, axis=-1, keepdims=True) + jnp.exp(sink - m))
    return jnp.einsum('bnhgqk,bnkhd->bnqhgd', attn.astype(vb.dtype), vb)


def _mem_attention(q, mk, mv):
    s = jnp.einsum('bthd,bmhd->bhtm', q, mk).astype(F32) * (MEM_HEAD_DIM ** -0.5)
    attn = jax.nn.softmax(s, axis=-1).astype(mv.dtype)
    return jnp.einsum('bhtm,bmhd->bthd', attn, mv)


def _mem_kv(mem, g_mem, w_mem_kv, g_km):
    b, m, _ = mem.shape
    k, v = jnp.split(_rmsnorm(mem, g_mem) @ w_mem_kv, 2, axis=-1)
    k = _rmsnorm(k.reshape(b, m, MEM_HEADS, MEM_HEAD_DIM), g_km)
    return k, v.reshape(b, m, MEM_HEADS, MEM_HEAD_DIM)


def _peer(h, w_q, sub_keys, u_tab, v_tab):
    T = h.shape[0]
    q = (h @ w_q).reshape(T, PEER_HEADS, 2, PEER_HALF)
    s = jnp.einsum('thcd,hcnd->thcn', q, sub_keys).astype(F32)
    v_top, i_top = lax.top_k(s, PEER_TOPK)
    cand = (v_top[:, :, 0, :, None] + v_top[:, :, 1, None, :]).reshape(T, PEER_HEADS, PEER_TOPK * PEER_TOPK)
    cand_idx = (i_top[:, :, 0, :, None] * N_KEYS + i_top[:, :, 1, None, :]).reshape(T, PEER_HEADS, -1)
    sc, pos = lax.top_k(cand, PEER_TOPK)
    idx = jnp.take_along_axis(cand_idx, pos, axis=-1)
    gates = jax.nn.softmax(sc, axis=-1)
    pad = (-T) % PEER_TOKEN_BLOCK
    nb = (T + pad) // PEER_TOKEN_BLOCK
    hb = jnp.pad(h, ((0, pad), (0, 0))).reshape(nb, PEER_TOKEN_BLOCK, D_MODEL)
    ib = jnp.pad(idx, ((0, pad), (0, 0), (0, 0))).reshape(nb, PEER_TOKEN_BLOCK, PEER_HEADS, PEER_TOPK)
    gb = jnp.pad(gates, ((0, pad), (0, 0), (0, 0))).reshape(nb, PEER_TOKEN_BLOCK, PEER_HEADS, PEER_TOPK)

    def block(args):
        hx, ix, gx = args
        a = jnp.einsum('thkd,td->thk', u_tab[ix], hx).astype(F32)
        w = (gx * jax.nn.gelu(a, approximate=False)).astype(hx.dtype)
        return jnp.einsum('thk,thkd->td', w, v_tab[ix])

    out = lax.map(block, (hb, ib, gb))
    return out.reshape(nb * PEER_TOKEN_BLOCK, D_MODEL)[:T]


def _layer(x, mem_k, mem_v, conv_buf, ssm_state, win_k, win_v, slopes, p):
    B, L, _ = x.shape
    kvh, gq, hd = SWA_KV_HEADS, SWA_HEADS // SWA_KV_HEADS, SWA_HEAD_DIM
    h = _rmsnorm(x, p['g_mix'])
    z, xbc, dt_raw, q, k, v, qm, gate_pre = _split_columns(h @ p['w_in'])

    xbc, new_conv = _causal_dwconv(xbc, conv_buf, p['conv_w'], p['conv_b'])
    xbc = jax.nn.silu(xbc)
    gn = SSD_GROUPS * D_STATE
    xs = xbc[..., :D_INNER].reshape(B, L, SSD_HEADS, SSD_HEAD_DIM)
    bm = xbc[..., D_INNER:D_INNER + gn].reshape(B, L, SSD_GROUPS, D_STATE)
    cm = xbc[..., D_INNER + gn:].reshape(B, L, SSD_GROUPS, D_STATE)
    dt = jax.nn.softplus((dt_raw + p['dt_bias']).astype(F32))
    A = -jnp.exp(p['a_log'].astype(F32))
    y, new_ssm = _ssd_scan(xs, dt, A, bm, cm, ssm_state)
    y = y.astype(x.dtype) + p['d_skip'][:, None] * xs
    y = y.reshape(B, L, D_INNER) * jax.nn.silu(z)
    y = _rmsnorm(y.reshape(B, L, SSD_GROUPS, D_INNER // SSD_GROUPS),
                 p['g_ssd_norm'].reshape(SSD_GROUPS, D_INNER // SSD_GROUPS)).reshape(B, L, D_INNER)
    br_ssm = y @ p['w_o_ssm']

    q = _rmsnorm(q.reshape(B, L, SWA_HEADS, hd), p['g_q'])
    k = _rmsnorm(k.reshape(B, L, kvh, hd), p['g_k'])
    v = v.reshape(B, L, kvh, hd)
    if win_k is None:
        nb = L // WINDOW
        qb = q.reshape(B, nb, WINDOW, kvh, gq, hd)
        kc = k.reshape(B, nb, WINDOW, kvh, hd)
        vc = v.reshape(B, nb, WINDOW, kvh, hd)
        kb = jnp.concatenate([jnp.concatenate([jnp.zeros_like(kc[:, :1]), kc[:, :-1]], axis=1), kc], axis=2)
        vb = jnp.concatenate([jnp.concatenate([jnp.zeros_like(vc[:, :1]), vc[:, :-1]], axis=1), vc], axis=2)
        valid = (jnp.arange(nb)[:, None] > 0) | (jnp.arange(2 * WINDOW)[None, :] >= WINDOW)
        new_wk, new_wv = k[:, -WINDOW:], v[:, -WINDOW:]
    else:
        qb = q.reshape(B, 1, L, kvh, gq, hd)
        k_all = jnp.concatenate([win_k.astype(k.dtype), k], axis=1)
        v_all = jnp.concatenate([win_v.astype(v.dtype), v], axis=1)
        kb, vb = k_all[:, None], v_all[:, None]
        valid = jnp.ones((1, WINDOW + L), bool)
        new_wk, new_wv = k_all[:, -WINDOW:], v_all[:, -WINDOW:]
    o = _window_attention(qb, kb, vb, valid, p['sinks'].astype(F32).reshape(kvh, gq),
                          slopes.reshape(kvh, gq)).reshape(B, L, SWA_HEADS * hd)
    br_swa = o @ p['w_o_swa']

    qm = _rmsnorm(qm.reshape(B, L, MEM_HEADS, MEM_HEAD_DIM), p['g_qm'])
    br_mem = _mem_attention(qm, mem_k.astype(qm.dtype), mem_v.astype(qm.dtype)).reshape(B, L, -1) @ p['w_o_mem']

    g = jax.nn.sigmoid((gate_pre + p['b_gate']).astype(F32)).astype(x.dtype).reshape(B, L, N_BRANCH, D_MODEL)
    merged = g[:, :, 0] * br_ssm + g[:, :, 1] * br_swa + g[:, :, 2] * br_mem
    x = x + merged @ p['w_out']

    h2 = _rmsnorm(x, p['g_ffn']).reshape(B * L, D_MODEL)
    x = x + _peer(h2, p['w_peer_q'], p['peer_sub_keys'], p['peer_u'], p['peer_v']).reshape(B, L, D_MODEL)
    return x, new_conv, new_ssm.astype(x.dtype), new_wk, new_wv


def setup_inputs(seed: int = 0) -> dict:
    key = jax.random.key(seed)
    ks = jax.random.split(key, 40)
    nrm = lambda k, shape, scale: jax.random.normal(k, shape, F32) * scale
    gain = lambda k, shape: 1.0 + 0.02 * jax.random.normal(k, shape, F32)
    dt0 = jnp.exp(jax.random.uniform(ks[13], (DEPTH, SSD_HEADS), F32, math.log(1e-3), math.log(1e-1)))
    return {
        'x_prompt': nrm(ks[0], (BATCH, SEQ, D_MODEL), 1.0),
        'x_sample': nrm(ks[1], (DEC_BATCH, DEC_SEQ, D_MODEL), 1.0),
        'cache_win_k': nrm(ks[2], (DEPTH, DEC_BATCH, WINDOW, SWA_KV_HEADS, SWA_HEAD_DIM), 1.0),
        'cache_win_v': nrm(ks[3], (DEPTH, DEC_BATCH, WINDOW, SWA_KV_HEADS, SWA_HEAD_DIM), 1.0),
        'state_ssm': nrm(ks[4], (DEPTH, DEC_BATCH, SSD_HEADS, SSD_HEAD_DIM, D_STATE), 1.0),
        'state_conv': nrm(ks[5], (DEPTH, DEC_BATCH, CONV_W - 1, CONV_DIM), 1.0),
        'cache_mem_k': nrm(ks[6], (DEPTH, DEC_BATCH, N_MEM, MEM_HEADS, MEM_HEAD_DIM), 1.0),
        'cache_mem_v': nrm(ks[7], (DEPTH, DEC_BATCH, N_MEM, MEM_HEADS, MEM_HEAD_DIM), 1.0),
        'mem_prompt': nrm(ks[8], (BATCH, N_MEM, D_MODEL), 1.0),
        'w_in': nrm(ks[9], (DEPTH, D_MODEL, IN_DIM), D_MODEL ** -0.5),
        'b_gate': nrm(ks[10], (DEPTH, N_BRANCH * D_MODEL), 0.02),
        'conv_w': nrm(ks[11], (DEPTH, CONV_W, CONV_DIM), CONV_W ** -0.5),
        'conv_b': nrm(ks[12], (DEPTH, CONV_DIM), 0.02),
        'dt_bias': dt0 + jnp.log(-jnp.expm1(-dt0)),
        'a_log': jnp.log(jax.random.uniform(ks[14], (DEPTH, SSD_HEADS), F32, 1.0, 16.0)),
        'd_skip': gain(ks[15], (DEPTH, SSD_HEADS)),
        'g_ssd_norm': gain(ks[16], (DEPTH, D_INNER)),
        'g_q': gain(ks[17], (DEPTH, SWA_HEAD_DIM)),
        'g_k': gain(ks[18], (DEPTH, SWA_HEAD_DIM)),
        'attn_sinks': nrm(ks[19], (DEPTH, SWA_HEADS), 0.5),
        'g_mem': gain(ks[20], (DEPTH, D_MODEL)),
        'w_mem_kv': nrm(ks[21], (DEPTH, D_MODEL, 2 * MEM_HEADS * MEM_HEAD_DIM), D_MODEL ** -0.5),
        'g_qm': gain(ks[22], (DEPTH, MEM_HEAD_DIM)),
        'g_km': gain(ks[23], (DEPTH, MEM_HEAD_DIM)),
        'w_o_ssm': nrm(ks[24], (DEPTH, D_INNER, D_MODEL), D_INNER ** -0.5),
        'w_o_swa': nrm(ks[25], (DEPTH, SWA_HEADS * SWA_HEAD_DIM, D_MODEL), (SWA_HEADS * SWA_HEAD_DIM) ** -0.5),
        'w_o_mem': nrm(ks[26], (DEPTH, MEM_HEADS * MEM_HEAD_DIM, D_MODEL), (MEM_HEADS * MEM_HEAD_DIM) ** -0.5),
        'w_out': nrm(ks[27], (DEPTH, D_MODEL, D_MODEL), D_MODEL ** -0.5),
        'g_mix': gain(ks[28], (DEPTH, D_MODEL)),
        'g_ffn': gain(ks[29], (DEPTH, D_MODEL)),
        'w_peer_q': nrm(ks[30], (DEPTH, D_MODEL, PEER_HEADS * PEER_QUERY_DIM), D_MODEL ** -0.5),
        'peer_sub_keys': nrm(ks[31], (DEPTH, PEER_HEADS, 2, N_KEYS, PEER_HALF), PEER_HALF ** -0.5),
        'peer_u': nrm(ks[32], (DEPTH, N_EXPERTS, D_MODEL), D_MODEL ** -0.5),
        'peer_v': nrm(ks[33], (DEPTH, N_EXPERTS, D_MODEL), (PEER_HEADS * PEER_TOPK) ** -0.5),
    }


def reference(x_prompt, x_sample, cache_win_k, cache_win_v, state_ssm, state_conv, cache_mem_k, cache_mem_v,
              mem_prompt, w_in, b_gate, conv_w, conv_b, dt_bias, a_log, d_skip, g_ssd_norm, g_q, g_k,
              attn_sinks, g_mem, w_mem_kv, g_qm, g_km, w_o_ssm, w_o_swa, w_o_mem, w_out, g_mix, g_ffn,
              w_peer_q, peer_sub_keys, peer_u, peer_v):
    slopes = _alibi_slopes(SWA_HEADS)
    bp = x_prompt.shape[0]
    yp, ys = x_prompt, x_sample
    p_wk, p_wv, p_ssm, p_conv, p_mk, p_mv = [], [], [], [], [], []
    s_wk, s_wv, s_ssm, s_conv = [], [], [], []
    for l in range(DEPTH):
        prm = {'w_in': w_in[l], 'b_gate': b_gate[l], 'conv_w': conv_w[l], 'conv_b': conv_b[l],
               'dt_bias': dt_bias[l], 'a_log': a_log[l], 'd_skip': d_skip[l], 'g_ssd_norm': g_ssd_norm[l],
               'g_q': g_q[l], 'g_k': g_k[l], 'sinks': attn_sinks[l], 'g_qm': g_qm[l],
               'w_o_ssm': w_o_ssm[l], 'w_o_swa': w_o_swa[l], 'w_o_mem': w_o_mem[l], 'w_out': w_out[l],
               'g_mix': g_mix[l], 'g_ffn': g_ffn[l], 'w_peer_q': w_peer_q[l],
               'peer_sub_keys': peer_sub_keys[l], 'peer_u': peer_u[l], 'peer_v': peer_v[l]}
        mk, mv = _mem_kv(mem_prompt, g_mem[l], w_mem_kv[l], g_km[l])
        conv0 = jnp.zeros((bp, CONV_W - 1, CONV_DIM), x_prompt.dtype)
        ssm0 = jnp.zeros((bp, SSD_HEADS, SSD_HEAD_DIM, D_STATE), x_prompt.dtype)
        yp, pc, ps, pk, pv = _layer(yp, mk, mv, conv0, ssm0, None, None, slopes, prm)
        p_wk.append(pk); p_wv.append(pv); p_ssm.append(ps); p_conv.append(pc); p_mk.append(mk); p_mv.append(mv)
        ys, sc, ss, sk, sv = _layer(ys, cache_mem_k[l], cache_mem_v[l], state_conv[l], state_ssm[l],
                                    cache_win_k[l], cache_win_v[l], slopes, prm)
        s_wk.append(sk); s_wv.append(sv); s_ssm.append(ss); s_conv.append(sc)
    return (yp, ys, jnp.stack(p_wk), jnp.stack(p_wv), jnp.stack(p_ssm), jnp.stack(p_conv),
            jnp.stack(p_mk), jnp.stack(p_mv), jnp.stack(s_wk), jnp.stack(s_wv), jnp.stack(s_ssm),
            jnp.stack(s_conv))
```

```python
import functools
import math

import jax
import jax.numpy as jnp
from jax import lax
from jax.experimental import pallas as pl
from jax.experimental.pallas import tpu as pltpu

F32 = jnp.float32
BF16 = jnp.bfloat16

D_MODEL = 2048
BATCH = 4
SEQ = 2048
DEPTH = 2
DEC_BATCH = 32
DEC_SEQ = 8
D_INNER = D_MODEL
SSD_HEAD_DIM = 64
SSD_HEADS = D_INNER // SSD_HEAD_DIM
SSD_GROUPS = 4
D_STATE = 128
CONV_W = 4
CONV_DIM = D_INNER + 2 * SSD_GROUPS * D_STATE
SSD_CHUNK = 128
SWA_HEAD_DIM = 64
SWA_HEADS = D_MODEL // SWA_HEAD_DIM
SWA_KV_HEADS = SWA_HEADS // 8
WINDOW = 128
N_MEM = 256
MEM_HEADS = 4
MEM_HEAD_DIM = D_MODEL // MEM_HEADS
PEER_HEADS = 8
N_KEYS = 128
N_EXPERTS = N_KEYS * N_KEYS
PEER_QUERY_DIM = 256
PEER_HALF = PEER_QUERY_DIM // 2
PEER_TOPK = 16
N_BRANCH = 3
EPS = 1e-6

LANES = 128
SUBLANES = 8
VMEM_LIMIT = 56 * 1024 * 1024

T_PROMPT = BATCH * SEQ
T_SAMPLE = DEC_BATCH * DEC_SEQ
T_ALL = T_PROMPT + T_SAMPLE
TM = 768

COL_XBC = 0
COL_K = CONV_DIM
COL_V = COL_K + 256
COL_DT = COL_V + 256
COL_Z = 4096
COL_Q = COL_Z + D_INNER
COL_QM = COL_Q + D_MODEL
COL_GATE = COL_QM + D_MODEL
IN_PAD = COL_GATE + N_BRANCH * D_MODEL
TN_IN = 512

NEG_INF = float("-inf")


def _cparams(sem):
    return pltpu.CompilerParams(dimension_semantics=sem, vmem_limit_bytes=VMEM_LIMIT)


def _nt_dot(a, b):
    return lax.dot_general(a, b, (((1,), (1,)), ((), ())), preferred_element_type=F32)


def _tn_dot(a, b):
    return lax.dot_general(a, b, (((0,), (0,)), ((), ())), preferred_element_type=F32)


def _norm_matmul_kernel(x_ref, g_ref, w_ref, gn_ref, o_ref, h_ref, *, norm_lo, norm_hi):
    n = pl.program_id(1)

    @pl.when(n == 0)
    def _():
        x = x_ref[...]
        ms = jnp.mean(x * x, axis=-1, keepdims=True)
        h_ref[...] = (x * lax.rsqrt(ms + EPS) * g_ref[...]).astype(BF16)

    y = jnp.dot(h_ref[...], w_ref[...], preferred_element_type=F32)
    is_norm = jnp.logical_and(n >= norm_lo, n < norm_hi)

    @pl.when(is_norm)
    def _():
        ms = jnp.mean(y * y, axis=-1, keepdims=True)
        o_ref[...] = y * lax.rsqrt(ms + EPS) * gn_ref[...]

    @pl.when(jnp.logical_not(is_norm))
    def _():
        o_ref[...] = y


def _norm_matmul(x, g, w, gn, *, tm, tn, norm_lo, norm_hi):
    m, k = x.shape
    n = w.shape[1]
    return pl.pallas_call(
        functools.partial(_norm_matmul_kernel, norm_lo=norm_lo, norm_hi=norm_hi),
        grid=(m // tm, n // tn),
        in_specs=[
            pl.BlockSpec((tm, k), lambda i, j: (i, 0)),
            pl.BlockSpec((1, k), lambda i, j: (0, 0)),
            pl.BlockSpec((k, tn), lambda i, j: (0, j)),
            pl.BlockSpec((1, tn), lambda i, j: (0, 0)),
        ],
        out_specs=pl.BlockSpec((tm, tn), lambda i, j: (i, j)),
        out_shape=jax.ShapeDtypeStruct((m, n), F32),
        scratch_shapes=[pltpu.VMEM((tm, k), BF16)],
        compiler_params=_cparams(("parallel", "arbitrary")),
        name="norm_matmul",
    )(x, g, w, gn)


def _ssd_kernel(xbc_ref, z_ref, dt_ref, cw_ref, cb_ref, dtb_ref, alog_ref, dskip_ref, gn_ref,
                conv0_ref, ssm0_ref, prev_ref, y_ref, ssm_out_ref, ext_ref, s_ref, *, q, nc):
    del prev_ref
    c = pl.program_id(1)

    @pl.when(c == 0)
    def _():
        ext_ref[0:SUBLANES, :] = conv0_ref[0]
        s_ref[...] = ssm0_ref[0]

    ext_ref[SUBLANES:SUBLANES + q, :] = xbc_ref[...]
    w = cw_ref[...]
    conv = (cb_ref[...]
            + w[3:4] * ext_ref[SUBLANES:SUBLANES + q, :]
            + w[2:3] * ext_ref[SUBLANES - 1:SUBLANES - 1 + q, :]
            + w[1:2] * ext_ref[SUBLANES - 2:SUBLANES - 2 + q, :]
            + w[0:1] * ext_ref[SUBLANES - 3:SUBLANES - 3 + q, :])
    tail = ext_ref[q:q + SUBLANES, :]
    ext_ref[0:SUBLANES, :] = tail
    xc = conv * jax.nn.sigmoid(conv)

    gn_w = SSD_GROUPS * D_STATE
    bm = xc[:, D_INNER:D_INNER + gn_w].astype(BF16)
    cm = xc[:, D_INNER + gn_w:].astype(BF16)

    dtr = dt_ref[...] + dtb_ref[...]
    dt = jnp.maximum(dtr, 0.0) + jnp.log1p(jnp.exp(-jnp.abs(dtr)))
    a_neg = -jnp.exp(alog_ref[...])
    da = dt * a_neg
    row = lax.broadcasted_iota(jnp.int32, (q, q), 0)
    col = lax.broadcasted_iota(jnp.int32, (q, q), 1)
    causal = row >= col
    tri = causal.astype(F32)
    cum = jnp.dot(tri, da, preferred_element_type=F32, precision=lax.Precision.HIGHEST)
    if q < LANES:
        cum_sq = jnp.concatenate([cum, jnp.zeros((LANES - q, LANES), F32)], axis=0)
    else:
        cum_sq = cum
    cum_t = cum_sq.T[:, :q]
    ecum = jnp.exp(cum)
    cl = cum[q - 1:q, :]
    wend = jnp.exp(cl - cum)
    ecl = jnp.exp(cl)

    lo = lax.broadcasted_iota(jnp.int32, (q, LANES), 1) < SSD_HEAD_DIM
    row_lo = lax.broadcasted_iota(jnp.int32, (LANES, LANES), 0) < SSD_HEAD_DIM
    pairs_per_group = SSD_HEADS // SSD_GROUPS // 2

    for g in range(SSD_GROUPS):
        bg = bm[:, g * D_STATE:(g + 1) * D_STATE]
        cg = cm[:, g * D_STATE:(g + 1) * D_STATE]
        cb = _nt_dot(cg, bg)
        y_parts = []
        for pp in range(pairs_per_group):
            hp = g * pairs_per_group + pp
            ha, hb = 2 * hp, 2 * hp + 1
            sl = slice(hp * LANES, (hp + 1) * LANES)
            xp = xc[:, sl]
            xdt = xp * jnp.where(lo, dt[:, ha:ha + 1], dt[:, hb:hb + 1])
            xdt_b = xdt.astype(BF16)
            seg_a = cum[:, ha:ha + 1] - cum_t[ha:ha + 1, :]
            seg_b = cum[:, hb:hb + 1] - cum_t[hb:hb + 1, :]
            m_a = (cb * jnp.exp(jnp.where(causal, seg_a, NEG_INF))).astype(BF16)
            m_b = (cb * jnp.exp(jnp.where(causal, seg_b, NEG_INF))).astype(BF16)
            y_a = jnp.dot(m_a, xdt_b, preferred_element_type=F32)
            y_b = jnp.dot(m_b, xdt_b, preferred_element_type=F32)
            y = jnp.where(lo, y_a, y_b)
            sp = s_ref[hp]
            cs = _nt_dot(cg, sp.astype(BF16))
            y = y + cs * jnp.where(lo, ecum[:, ha:ha + 1], ecum[:, hb:hb + 1])
            xw = (xdt * jnp.where(lo, wend[:, ha:ha + 1], wend[:, hb:hb + 1])).astype(BF16)
            upd = _tn_dot(xw, bg)
            dec = jnp.where(row_lo, ecl[:, ha:ha + 1], ecl[:, hb:hb + 1])
            s_ref[hp] = sp * dec + upd
            y = y + dskip_ref[:, sl] * xp
            zz = z_ref[:, sl]
            y = y * (zz * jax.nn.sigmoid(zz))
            y_parts.append(y)
        yg = jnp.concatenate(y_parts, axis=1)
        ms = jnp.mean(yg * yg, axis=-1, keepdims=True)
        gw = D_INNER // SSD_GROUPS
        y_ref[:, g * gw:(g + 1) * gw] = yg * lax.rsqrt(ms + EPS) * gn_ref[:, g * gw:(g + 1) * gw]

    @pl.when(c == nc - 1)
    def _():
        ssm_out_ref[0] = s_ref[...]


def _ssd(y_in, cw, cb, dtb, alog, dskip, gn, conv0, ssm0, prev, *, nb, q, nc, row0):
    rb = row0 // q
    in_specs = [
        pl.BlockSpec((q, CONV_DIM), lambda b, c: (rb + b * nc + c, COL_XBC // CONV_DIM)),
        pl.BlockSpec((q, D_INNER), lambda b, c: (rb + b * nc + c, COL_Z // D_INNER)),
        pl.BlockSpec((q, LANES), lambda b, c: (rb + b * nc + c, COL_DT // LANES)),
        pl.BlockSpec((CONV_W, CONV_DIM), lambda b, c: (0, 0)),
        pl.BlockSpec((1, CONV_DIM), lambda b, c: (0, 0)),
        pl.BlockSpec((1, LANES), lambda b, c: (0, 0)),
        pl.BlockSpec((1, LANES), lambda b, c: (0, 0)),
        pl.BlockSpec((1, D_INNER), lambda b, c: (0, 0)),
        pl.BlockSpec((1, D_INNER), lambda b, c: (0, 0)),
        pl.BlockSpec((1, SUBLANES, CONV_DIM), lambda b, c: (b, 0, 0)),
        pl.BlockSpec((1, SSD_HEADS // 2, LANES, D_STATE), lambda b, c: (b, 0, 0, 0)),
    ]
    args = [y_in, y_in, y_in, cw, cb, dtb, alog, dskip, gn, conv0, ssm0]
    aliases = {}
    if prev is not None:
        in_specs.append(pl.BlockSpec(memory_space=pl.ANY))
        args.append(prev)
        aliases = {len(args) - 1: 0}
    else:
        in_specs.append(pl.BlockSpec((1, LANES), lambda b, c: (0, 0)))
        args.append(dtb)
    return pl.pallas_call(
        functools.partial(_ssd_kernel, q=q, nc=nc),
        grid=(nb, nc),
        in_specs=in_specs,
        out_specs=[
            pl.BlockSpec((q, D_INNER), lambda b, c: (rb + b * nc + c, 0)),
            pl.BlockSpec((1, SSD_HEADS // 2, LANES, D_STATE), lambda b, c: (b, 0, 0, 0)),
        ],
        out_shape=[
            jax.ShapeDtypeStruct((T_ALL, D_INNER), F32),
            jax.ShapeDtypeStruct((nb, SSD_HEADS // 2, LANES, D_STATE), F32),
        ],
        scratch_shapes=[
            pltpu.VMEM((q + SUBLANES, CONV_DIM), F32),
            pltpu.VMEM((SSD_HEADS // 2, LANES, D_STATE), F32),
        ],
        input_output_aliases=aliases,
        compiler_params=_cparams(("parallel", "arbitrary")),
        name="ssd_q%d" % q,
    )(*args)


def _half_rmsnorm(x, gain, lo):
    sq = x * x
    ms_lo = jnp.sum(jnp.where(lo, sq, 0.0), axis=-1, keepdims=True) * (1.0 / SWA_HEAD_DIM)
    ms_hi = jnp.sum(jnp.where(lo, 0.0, sq), axis=-1, keepdims=True) * (1.0 / SWA_HEAD_DIM)
    return x * jnp.where(lo, lax.rsqrt(ms_lo + EPS), lax.rsqrt(ms_hi + EPS)) * gain


def _both_halves(x, lo, take_lo):
    r = pltpu.roll(x, SWA_HEAD_DIM, axis=1)
    return jnp.where(lo, x, r) if take_lo else jnp.where(lo, r, x)


def _pad_rows(x, rows):
    if x.shape[0] == rows:
        return x
    return jnp.concatenate([x, jnp.zeros((rows - x.shape[0], x.shape[1]), x.dtype)], axis=0)


def _swa_kernel(sink_ref, slope_ref, q_ref, kc_ref, vc_ref, kp_ref, vp_ref, gq_ref, gk_ref, prev_ref,
                o_ref, kn_ref, *, lq, norm_prev, first_block_has_no_prev):
    del prev_ref
    gq_heads = SWA_HEADS // SWA_KV_HEADS
    rows = gq_heads * lq
    lo_q = lax.broadcasted_iota(jnp.int32, (lq, LANES), 1) < SWA_HEAD_DIM
    lo_w = lax.broadcasted_iota(jnp.int32, (WINDOW, LANES), 1) < SWA_HEAD_DIM
    gq = gq_ref[...]
    gk = gk_ref[...]

    kc = kc_ref[...]
    vc = vc_ref[...]
    if len(kp_ref.shape) == 3:
        kp, vp = kp_ref[0], vp_ref[0]
    else:
        kp, vp = kp_ref[...], vp_ref[...]
    kn_slabs, kp_slabs = [], []
    for s in range(2):
        sl = slice(s * LANES, (s + 1) * LANES)
        kn_s = _half_rmsnorm(kc[:, sl], gk, lo_q)
        kn_ref[:, sl] = kn_s
        kn_slabs.append(kn_s)
        kp_slabs.append(_half_rmsnorm(kp[:, sl], gk, lo_w) if norm_prev else kp[:, sl])

    r_idx = lax.broadcasted_iota(jnp.int32, (rows, WINDOW), 0)
    j_idx = lax.broadcasted_iota(jnp.int32, (rows, WINDOW), 1)
    t_idx = r_idx % lq
    mask_prev = j_idx >= t_idx
    if first_block_has_no_prev:
        mask_prev = jnp.logical_and(mask_prev, pl.program_id(1) > 0)
    mask_cur = j_idx <= t_idx
    dist_prev = (t_idx + WINDOW - j_idx).astype(F32)
    dist_cur = (t_idx - j_idx).astype(F32)
    head_of_row = lax.broadcasted_iota(jnp.int32, (rows, 1), 0) // lq
    scale = SWA_HEAD_DIM ** -0.5

    for kvh in range(SWA_KV_HEADS):
        slab, take_lo = kvh // 2, (kvh % 2 == 0)
        sl = slice(slab * LANES, (slab + 1) * LANES)
        k_cur = _pad_rows(_both_halves(kn_slabs[slab], lo_q, take_lo), WINDOW).astype(BF16)
        v_cur = _pad_rows(_both_halves(vc[:, sl], lo_q, take_lo), WINDOW).astype(BF16)
        k_prev = _both_halves(kp_slabs[slab], lo_w, take_lo).astype(BF16)
        v_prev = _both_halves(vp[:, sl], lo_w, take_lo).astype(BF16)

        q_parts = []
        for p in range(gq_heads // 2):
            c0 = kvh * gq_heads * SWA_HEAD_DIM + p * LANES
            qn = _half_rmsnorm(q_ref[:, c0:c0 + LANES], gq, lo_q)
            q_parts.append(jnp.where(lo_q, qn, 0.0))
            q_parts.append(jnp.where(lo_q, 0.0, qn))
        qs = jnp.concatenate(q_parts, axis=0).astype(BF16)

        sink = jnp.zeros((rows, 1), F32)
        slope = jnp.zeros((rows, 1), F32)
        for gi in range(gq_heads):
            sink = jnp.where(head_of_row == gi, sink_ref[kvh * gq_heads + gi], sink)
            slope = jnp.where(head_of_row == gi, slope_ref[kvh * gq_heads + gi], slope)

        s_prev = _nt_dot(qs, k_prev) * scale - slope * dist_prev
        s_cur = _nt_dot(qs, k_cur) * scale - slope * dist_cur
        s_prev = jnp.where(mask_prev, s_prev, NEG_INF)
        s_cur = jnp.where(mask_cur, s_cur, NEG_INF)
        m = jnp.maximum(jnp.maximum(jnp.max(s_prev, axis=-1, keepdims=True),
                                    jnp.max(s_cur, axis=-1, keepdims=True)), sink)
        p_prev = jnp.exp(s_prev - m)
        p_cur = jnp.exp(s_cur - m)
        den = (jnp.sum(p_prev, axis=-1, keepdims=True) + jnp.sum(p_cur, axis=-1, keepdims=True)
               + jnp.exp(sink - m))
        o = (jnp.dot(p_prev.astype(BF16), v_prev, preferred_element_type=F32)
             + jnp.dot(p_cur.astype(BF16), v_cur, preferred_element_type=F32)) / den
        for p in range(gq_heads // 2):
            c0 = kvh * gq_heads * SWA_HEAD_DIM + p * LANES
            o_lo = o[(2 * p) * lq:(2 * p + 1) * lq]
            o_hi = o[(2 * p + 1) * lq:(2 * p + 2) * lq]
            o_ref[:, c0:c0 + LANES] = jnp.where(lo_q, o_lo, o_hi)


def _swa(y_in, sinks, slopes, gq2, gk2, win_k, win_v, prev, *, nb, lq, nq, row0):
    rb = row0 // lq
    kw = SWA_KV_HEADS * SWA_HEAD_DIM
    in_specs = [
        pl.BlockSpec(memory_space=pltpu.SMEM),
        pl.BlockSpec(memory_space=pltpu.SMEM),
        pl.BlockSpec((lq, D_MODEL), lambda b, i: (rb + b * nq + i, COL_Q // D_MODEL)),
        pl.BlockSpec((lq, kw), lambda b, i: (rb + b * nq + i, COL_K // kw)),
        pl.BlockSpec((lq, kw), lambda b, i: (rb + b * nq + i, COL_V // kw)),
    ]
    args = [sinks, slopes, y_in, y_in, y_in]
    if win_k is None:
        in_specs += [
            pl.BlockSpec((lq, kw), lambda b, i: (rb + b * nq + jnp.maximum(i - 1, 0), COL_K // kw)),
            pl.BlockSpec((lq, kw), lambda b, i: (rb + b * nq + jnp.maximum(i - 1, 0), COL_V // kw)),
        ]
        args += [y_in, y_in]
    else:
        in_specs += [
            pl.BlockSpec((1, WINDOW, kw), lambda b, i: (b, 0, 0)),
            pl.BlockSpec((1, WINDOW, kw), lambda b, i: (b, 0, 0)),
        ]
        args += [win_k, win_v]
    in_specs += [pl.BlockSpec((1, LANES), lambda b, i: (0, 0)), pl.BlockSpec((1, LANES), lambda b, i: (0, 0))]
    args += [gq2, gk2]
    aliases = {}
    if prev is not None:
        in_specs.append(pl.BlockSpec(memory_space=pl.ANY))
        args.append(prev)
        aliases = {len(args) - 1: 0}
    else:
        in_specs.append(pl.BlockSpec((1, LANES), lambda b, i: (0, 0)))
        args.append(gq2)
    return pl.pallas_call(
        functools.partial(_swa_kernel, lq=lq, norm_prev=win_k is None, first_block_has_no_prev=win_k is None),
        grid=(nb, nq),
        in_specs=in_specs,
        out_specs=[
            pl.BlockSpec((lq, D_MODEL), lambda b, i: (rb + b * nq + i, 0)),
            pl.BlockSpec((lq, kw), lambda b, i: (b * nq + i, 0)),
        ],
        out_shape=[
            jax.ShapeDtypeStruct((T_ALL, D_MODEL), F32),
            jax.ShapeDtypeStruct((nb * nq * lq, kw), F32),
        ],
        input_output_aliases=aliases,
        compiler_params=_cparams(("parallel", "arbitrary")),
        name="swa_lq%d" % lq,
    )(*args)


def _mem_attn_kernel(q_ref, k_ref, v_ref, prev_ref, o_ref):
    del prev_ref
    scale = MEM_HEAD_DIM ** -0.5
    for h in range(MEM_HEADS):
        sl = slice(h * MEM_HEAD_DIM, (h + 1) * MEM_HEAD_DIM)
        qh = q_ref[:, sl].astype(BF16)
        kh = k_ref[0, :, sl].astype(BF16)
        vh = v_ref[0, :, sl].astype(BF16)
        s = _nt_dot(qh, kh) * scale
        m = jnp.max(s, axis=-1, keepdims=True)
        p = jnp.exp(s - m)
        den = jnp.sum(p, axis=-1, keepdims=True)
        o = jnp.dot(p.astype(BF16), vh, preferred_element_type=F32) / den
        o_ref[:, sl] = o


def _mem_attn(y_in, mem_k, mem_v, prev, *, nb, tq, nq, row0):
    rb = row0 // tq
    in_specs = [
        pl.BlockSpec((tq, D_MODEL), lambda b, i: (rb + b * nq + i, COL_QM // D_MODEL)),
        pl.BlockSpec((1, N_MEM, D_MODEL), lambda b, i: (b, 0, 0)),
        pl.BlockSpec((1, N_MEM, D_MODEL), lambda b, i: (b, 0, 0)),
    ]
    args = [y_in, mem_k, mem_v]
    aliases = {}
    if prev is not None:
        in_specs.append(pl.BlockSpec(memory_space=pl.ANY))
        args.append(prev)
        aliases = {len(args) - 1: 0}
    else:
        in_specs.append(pl.BlockSpec((1, N_MEM, D_MODEL), lambda b, i: (0, 0, 0)))
        args.append(mem_k)
    return pl.pallas_call(
        _mem_attn_kernel,
        grid=(nb, nq),
        in_specs=in_specs,
        out_specs=pl.BlockSpec((tq, D_MODEL), lambda b, i: (rb + b * nq + i, 0)),
        out_shape=jax.ShapeDtypeStruct((T_ALL, D_MODEL), F32),
        input_output_aliases=aliases,
        compiler_params=_cparams(("parallel", "arbitrary")),
        name="mem_attn_tq%d" % tq,
    )(*args)


def _merge_kernel(a0_ref, a1_ref, a2_ref, w0_ref, w1_ref, w2_ref, g0_ref, g1_ref, g2_ref,
                  b0_ref, b1_ref, b2_ref, o_ref):
    acc = None
    for a_ref, w_ref, g_ref, b_ref in ((a0_ref, w0_ref, g0_ref, b0_ref), (a1_ref, w1_ref, g1_ref, b1_ref),
                                       (a2_ref, w2_ref, g2_ref, b2_ref)):
        br = jnp.dot(a_ref[...].astype(BF16), w_ref[...], preferred_element_type=F32)
        t = jax.nn.sigmoid(g_ref[...] + b_ref[...]) * br
        acc = t if acc is None else acc + t
    o_ref[...] = acc.astype(BF16)


def _merge(a0, a1, a2, w0, w1, w2, y_in, b_gate, *, tm, tn):
    gb = COL_GATE // tn
    nbk = D_MODEL // tn
    a_spec = pl.BlockSpec((tm, D_MODEL), lambda i, j: (i, 0))
    w_spec = pl.BlockSpec((D_MODEL, tn), lambda i, j: (0, j))
    return pl.pallas_call(
        _merge_kernel,
        grid=(T_ALL // tm, nbk),
        in_specs=[a_spec, a_spec, a_spec, w_spec, w_spec, w_spec,
                  pl.BlockSpec((tm, tn), lambda i, j: (i, gb + j)),
                  pl.BlockSpec((tm, tn), lambda i, j: (i, gb + nbk + j)),
                  pl.BlockSpec((tm, tn), lambda i, j: (i, gb + 2 * nbk + j)),
                  pl.BlockSpec((1, tn), lambda i, j: (0, j)),
                  pl.BlockSpec((1, tn), lambda i, j: (0, nbk + j)),
                  pl.BlockSpec((1, tn), lambda i, j: (0, 2 * nbk + j))],
        out_specs=pl.BlockSpec((tm, tn), lambda i, j: (i, j)),
        out_shape=jax.ShapeDtypeStruct((T_ALL, D_MODEL), BF16),
        compiler_params=_cparams(("parallel", "arbitrary")),
        name="gated_merge",
    )(a0, a1, a2, w0, w1, w2, y_in, y_in, y_in, b_gate, b_gate, b_gate)


def _matmul_res_kernel(a_ref, w_ref, r_ref, o_ref):
    o_ref[...] = r_ref[...] + jnp.dot(a_ref[...], w_ref[...], preferred_element_type=F32)


def _matmul_res(a, w, r, *, tn):
    m, k = a.shape
    n = w.shape[1]
    return pl.pallas_call(
        _matmul_res_kernel,
        grid=(m // TM, n // tn),
        in_specs=[pl.BlockSpec((TM, k), lambda i, j: (i, 0)),
                  pl.BlockSpec((k, tn), lambda i, j: (0, j)),
                  pl.BlockSpec((TM, tn), lambda i, j: (i, j))],
        out_specs=pl.BlockSpec((TM, tn), lambda i, j: (i, j)),
        out_shape=jax.ShapeDtypeStruct((m, n), F32),
        compiler_params=_cparams(("parallel", "arbitrary")),
        name="matmul_residual",
    )(a, w, r)


def _top_values(s, k):
    vals = []
    cur = s
    for r in range(k):
        mx = jnp.max(cur, axis=0, keepdims=True)
        vals.append(mx)
        if r + 1 < k:
            cur = jnp.where(cur == mx, NEG_INF, cur)
    return vals


def _router_kernel(x_ref, g_ref, wqt_ref, keys_ref, h2_ref, s0_ref, s1_ref, e0_ref, e1_ref, tau_ref):
    x = x_ref[...]
    ms = jnp.mean(x * x, axis=-1, keepdims=True)
    h2 = (x * lax.rsqrt(ms + EPS) * g_ref[...]).astype(BF16)
    h2_ref[...] = h2
    qt = _nt_dot(wqt_ref[...], h2)
    taus = []
    for h in range(PEER_HEADS):
        sc = []
        tops = []
        for c in range(2):
            r0 = (2 * h + c) * PEER_HALF
            s = jnp.dot(keys_ref[2 * h + c], qt[r0:r0 + PEER_HALF, :].astype(BF16),
                        preferred_element_type=F32)
            sc.append(s)
            tops.append(_top_values(s, PEER_TOPK))
        a1 = jnp.concatenate(tops[1], axis=0)
        cand = jnp.concatenate([tops[0][r] + a1 for r in range(PEER_TOPK)], axis=0)
        best = _top_values(cand, PEER_TOPK)
        m = best[0]
        z = jnp.zeros_like(m)
        for r in range(PEER_TOPK):
            z = z + jnp.exp(best[r] - m)
        taus.append(best[PEER_TOPK - 1])
        s0_ref[h] = sc[0]
        s1_ref[h] = sc[1]
        e0_ref[h] = jnp.exp(sc[0] - tops[0][0]) / z
        e1_ref[h] = jnp.exp(sc[1] - tops[1][0])
    tau_ref[...] = jnp.concatenate(taus, axis=0)


def _router(x, g, wqt, keys, *, tmr):
    tok = pl.BlockSpec((PEER_HEADS, N_KEYS, tmr), lambda i: (0, 0, i))
    return pl.pallas_call(
        _router_kernel,
        grid=(T_ALL // tmr,),
        in_specs=[pl.BlockSpec((tmr, D_MODEL), lambda i: (i, 0)),
                  pl.BlockSpec((1, D_MODEL), lambda i: (0, 0)),
                  pl.BlockSpec((PEER_HEADS * PEER_QUERY_DIM, D_MODEL), lambda i: (0, 0)),
                  pl.BlockSpec((2 * PEER_HEADS, N_KEYS, PEER_HALF), lambda i: (0, 0, 0))],
        out_specs=[pl.BlockSpec((tmr, D_MODEL), lambda i: (i, 0)), tok, tok, tok, tok,
                   pl.BlockSpec((PEER_HEADS, tmr), lambda i: (0, i))],
        out_shape=[jax.ShapeDtypeStruct((T_ALL, D_MODEL), BF16)]
        + [jax.ShapeDtypeStruct((PEER_HEADS, N_KEYS, T_ALL), F32)] * 4
        + [jax.ShapeDtypeStruct((PEER_HEADS, T_ALL), F32)],
        compiler_params=_cparams(("parallel",)),
        name="peer_router",
    )(x, g, wqt, keys)


def _peer_dense_kernel(h2_ref, u_ref, vt_ref, s0_ref, s1_ref, e0_ref, e1_ref, tau_ref, x_ref,
                       o_ref, acc_ref, w_ref, *, te, ne):
    e = pl.program_id(1)

    @pl.when(e == 0)
    def _():
        acc_ref[...] = jnp.zeros_like(acc_ref)

    at = _nt_dot(u_ref[...], h2_ref[...])
    for ii in range(te // N_KEYS):
        i = e * (te // N_KEYS) + ii
        gate = None
        for h in range(PEER_HEADS):
            s0i = s0_ref[h, pl.ds(i, 1), :]
            e0i = e0_ref[h, pl.ds(i, 1), :]
            sel = (s0i + s1_ref[h]) >= tau_ref[h:h + 1, :]
            t = jnp.where(sel, e1_ref[h] * e0i, 0.0)
            gate = t if gate is None else gate + t
        a = at[ii * N_KEYS:(ii + 1) * N_KEYS, :]
        gelu = 0.5 * a * (1.0 + lax.erf(a * (2.0 ** -0.5)))
        w_ref[ii * N_KEYS:(ii + 1) * N_KEYS, :] = (gate * gelu).astype(BF16)
    acc_ref[...] += jnp.dot(vt_ref[...], w_ref[...], preferred_element_type=F32)

    @pl.when(e == ne - 1)
    def _():
        o_ref[...] = x_ref[...] + acc_ref[...].T


def _peer_dense(h2, u, vt, s0, s1, e0, e1, tau, x, *, te):
    ne = N_EXPERTS // te
    once = pl.Buffered(1)
    tok = pl.BlockSpec((PEER_HEADS, N_KEYS, TM), lambda i, e: (0, 0, i), pipeline_mode=once)
    return pl.pallas_call(
        functools.partial(_peer_dense_kernel, te=te, ne=ne),
        grid=(T_ALL // TM, ne),
        in_specs=[pl.BlockSpec((TM, D_MODEL), lambda i, e: (i, 0), pipeline_mode=once),
                  pl.BlockSpec((te, D_MODEL), lambda i, e: (e, 0)),
                  pl.BlockSpec((D_MODEL, te), lambda i, e: (0, e)),
                  tok, tok, tok, tok,
                  pl.BlockSpec((PEER_HEADS, TM), lambda i, e: (0, i), pipeline_mode=once),
                  pl.BlockSpec((TM, D_MODEL), lambda i, e: (i, 0), pipeline_mode=once)],
        out_specs=pl.BlockSpec((TM, D_MODEL), lambda i, e: (i, 0)),
        out_shape=jax.ShapeDtypeStruct((T_ALL, D_MODEL), F32),
        scratch_shapes=[pltpu.VMEM((D_MODEL, TM), F32), pltpu.VMEM((te, TM), BF16)],
        compiler_params=_cparams(("parallel", "arbitrary")),
        name="peer_dense",
    )(h2, u, vt, s0, s1, e0, e1, tau, x)


def _prep_w_in(w):
    z, xbc, dt, q, k, v, qm, gate = jnp.split(
        w, [D_INNER, D_INNER + CONV_DIM, D_INNER + CONV_DIM + SSD_HEADS,
            D_INNER + CONV_DIM + SSD_HEADS + D_MODEL,
            D_INNER + CONV_DIM + SSD_HEADS + D_MODEL + 256,
            D_INNER + CONV_DIM + SSD_HEADS + D_MODEL + 512,
            D_INNER + CONV_DIM + SSD_HEADS + 2 * D_MODEL + 512], axis=1)
    pad = jnp.zeros((D_MODEL, COL_Z - COL_DT - SSD_HEADS), w.dtype)
    return jnp.concatenate([xbc, k, v, dt, pad, z, q, qm, gate], axis=1).astype(BF16)


def _pad_lanes(v, n=LANES):
    return jnp.pad(v, (0, n - v.shape[0])).reshape(1, n)


def kernel(x_prompt, x_sample, cache_win_k, cache_win_v, state_ssm, state_conv, cache_mem_k, cache_mem_v,
           mem_prompt, w_in, b_gate, conv_w, conv_b, dt_bias, a_log, d_skip, g_ssd_norm, g_q, g_k,
           attn_sinks, g_mem, w_mem_kv, g_qm, g_km, w_o_ssm, w_o_swa, w_o_mem, w_out, g_mix, g_ffn,
           w_peer_q, peer_sub_keys, peer_u, peer_v):
    x = jnp.concatenate([x_prompt.reshape(T_PROMPT, D_MODEL), x_sample.reshape(T_SAMPLE, D_MODEL)], axis=0)
    slopes = jnp.exp2(-8.0 * jnp.arange(1, SWA_HEADS + 1, dtype=F32) / SWA_HEADS)
    mem_x = mem_prompt.reshape(BATCH * N_MEM, D_MODEL)
    kw = SWA_KV_HEADS * SWA_HEAD_DIM
    nc = SEQ // SSD_CHUNK
    zero_conv = jnp.zeros((BATCH, SUBLANES, CONV_DIM), F32)
    zero_ssm = jnp.zeros((BATCH, SSD_HEADS // 2, LANES, D_STATE), F32)

    outs = {k: [] for k in ("p_wk", "p_wv", "p_ssm", "p_conv", "p_mk", "p_mv", "s_wk", "s_wv", "s_ssm", "s_conv")}
    for l in range(DEPTH):
        w_in_l = _prep_w_in(w_in[l])
        row = lambda v: v.reshape(1, -1)
        mem_kv = _norm_matmul(mem_x, row(g_mem[l]), w_mem_kv[l].astype(BF16), row(g_km[l]),
                              tm=512, tn=MEM_HEAD_DIM, norm_lo=0, norm_hi=MEM_HEADS)
        mk = mem_kv[:, :D_MODEL].reshape(BATCH, N_MEM, D_MODEL)
        mv = mem_kv[:, D_MODEL:].reshape(BATCH, N_MEM, D_MODEL)
        outs["p_mk"].append(mk.reshape(BATCH, N_MEM, MEM_HEADS, MEM_HEAD_DIM))
        outs["p_mv"].append(mv.reshape(BATCH, N_MEM, MEM_HEADS, MEM_HEAD_DIM))

        y = _norm_matmul(x, row(g_mix[l]), w_in_l, row(g_qm[l]), tm=TM, tn=TN_IN,
                         norm_lo=COL_QM // TN_IN, norm_hi=COL_GATE // TN_IN)

        cw, cb = conv_w[l], row(conv_b[l])
        dtb, alog = _pad_lanes(dt_bias[l]), _pad_lanes(a_log[l])
        dskip = row(jnp.repeat(d_skip[l], SSD_HEAD_DIM))
        gn = row(g_ssd_norm[l])
        s_conv0 = jnp.pad(state_conv[l], ((0, 0), (SUBLANES - (CONV_W - 1), 0), (0, 0)))
        s_ssm0 = state_ssm[l].reshape(DEC_BATCH, SSD_HEADS // 2, LANES, D_STATE)
        y_ssm, p_ssm = _ssd(y, cw, cb, dtb, alog, dskip, gn, zero_conv, zero_ssm, None,
                            nb=BATCH, q=SSD_CHUNK, nc=nc, row0=0)
        y_ssm, s_ssm = _ssd(y, cw, cb, dtb, alog, dskip, gn, s_conv0, s_ssm0, y_ssm,
                            nb=DEC_BATCH, q=DEC_SEQ, nc=1, row0=T_PROMPT)
        outs["p_ssm"].append(p_ssm.reshape(BATCH, SSD_HEADS, SSD_HEAD_DIM, D_STATE))
        outs["s_ssm"].append(s_ssm.reshape(DEC_BATCH, SSD_HEADS, SSD_HEAD_DIM, D_STATE))
        outs["p_conv"].append(y[:T_PROMPT, :CONV_DIM].reshape(BATCH, SEQ, CONV_DIM)[:, SEQ - (CONV_W - 1):])
        outs["s_conv"].append(
            y[T_PROMPT:, :CONV_DIM].reshape(DEC_BATCH, DEC_SEQ, CONV_DIM)[:, DEC_SEQ - (CONV_W - 1):])

        gq2 = row(jnp.concatenate([g_q[l], g_q[l]]))
        gk2 = row(jnp.concatenate([g_k[l], g_k[l]]))
        wk_c = cache_win_k[l].reshape(DEC_BATCH, WINDOW, kw)
        wv_c = cache_win_v[l].reshape(DEC_BATCH, WINDOW, kw)
        o_swa, kn_p = _swa(y, attn_sinks[l], slopes, gq2, gk2, None, None, None,
                           nb=BATCH, lq=WINDOW, nq=SEQ // WINDOW, row0=0)
        o_swa, kn_s = _swa(y, attn_sinks[l], slopes, gq2, gk2, wk_c, wv_c, o_swa,
                           nb=DEC_BATCH, lq=DEC_SEQ, nq=1, row0=T_PROMPT)
        v_p = y[:T_PROMPT, COL_V:COL_V + kw].reshape(BATCH, SEQ, kw)
        v_s = y[T_PROMPT:, COL_V:COL_V + kw].reshape(DEC_BATCH, DEC_SEQ, kw)
        kv_shape = (SWA_KV_HEADS, SWA_HEAD_DIM)
        outs["p_wk"].append(kn_p.reshape(BATCH, SEQ, kw)[:, SEQ - WINDOW:].reshape(BATCH, WINDOW, *kv_shape))
        outs["p_wv"].append(v_p[:, SEQ - WINDOW:].reshape(BATCH, WINDOW, *kv_shape))
        outs["s_wk"].append(jnp.concatenate([wk_c[:, DEC_SEQ:], kn_s.reshape(DEC_BATCH, DEC_SEQ, kw)], axis=1)
                            .reshape(DEC_BATCH, WINDOW, *kv_shape))
        outs["s_wv"].append(jnp.concatenate([wv_c[:, DEC_SEQ:], v_s], axis=1).reshape(DEC_BATCH, WINDOW, *kv_shape))

        o_mem = _mem_attn(y, mk, mv, None, nb=BATCH, tq=512, nq=SEQ // 512, row0=0)
        o_mem = _mem_attn(y, cache_mem_k[l].reshape(DEC_BATCH, N_MEM, D_MODEL),
                          cache_mem_v[l].reshape(DEC_BATCH, N_MEM, D_MODEL), o_mem,
                          nb=DEC_BATCH, tq=DEC_SEQ, nq=1, row0=T_PROMPT)

        merged = _merge(y_ssm, o_swa, o_mem, w_o_ssm[l].astype(BF16), w_o_swa[l].astype(BF16),
                        w_o_mem[l].astype(BF16), y, row(b_gate[l]), tm=384, tn=256)
        x = _matmul_res(merged, w_out[l].astype(BF16), x, tn=512)

        keys = peer_sub_keys[l].reshape(2 * PEER_HEADS, N_KEYS, PEER_HALF).astype(BF16)
        h2, s0, s1, e0, e1, tau = _router(x, row(g_ffn[l]), w_peer_q[l].T.astype(BF16), keys, tmr=256)
        x = _peer_dense(h2, peer_u[l].astype(BF16), peer_v[l].T.astype(BF16), s0, s1, e0, e1, tau, x, te=256)

    yp = x[:T_PROMPT].reshape(BATCH, SEQ, D_MODEL)
    ys = x[T_PROMPT:].reshape(DEC_BATCH, DEC_SEQ, D_MODEL)
    st = lambda k: jnp.stack(outs[k])
    return (yp, ys, st("p_wk"), st("p_wv"), st("p_ssm"), st("p_conv"), st("p_mk"), st("p_mv"),
            st("s_wk"), st("s_wv"), st("s_ssm"), st("s_conv"))
```

```python
import functools

import jax
import jax.numpy as jnp
from jax import lax
from jax.experimental import pallas as pl
from jax.experimental.pallas import tpu as pltpu

F32 = jnp.float32
BF16 = jnp.bfloat16

D_MODEL = 2048
BATCH = 4
SEQ = 2048
DEPTH = 2
DEC_BATCH = 32
DEC_SEQ = 8
D_INNER = D_MODEL
SSD_HEAD_DIM = 64
SSD_HEADS = D_INNER // SSD_HEAD_DIM
SSD_GROUPS = 4
D_STATE = 128
CONV_W = 4
CONV_DIM = D_INNER + 2 * SSD_GROUPS * D_STATE
SSD_CHUNK = 128
SWA_HEAD_DIM = 64
SWA_HEADS = D_MODEL // SWA_HEAD_DIM
SWA_KV_HEADS = SWA_HEADS // 8
WINDOW = 128
N_MEM = 256
MEM_HEADS = 4
MEM_HEAD_DIM = D_MODEL // MEM_HEADS
PEER_HEADS = 8
N_KEYS = 128
N_EXPERTS = N_KEYS * N_KEYS
PEER_QUERY_DIM = 256
PEER_HALF = PEER_QUERY_DIM // 2
PEER_TOPK = 16
N_BRANCH = 3
EPS = 1e-6

LANES = 128
SUBLANES = 8
VMEM_LIMIT = 56 * 1024 * 1024

T_PROMPT = BATCH * SEQ
T_SAMPLE = DEC_BATCH * DEC_SEQ
T_ALL = T_PROMPT + T_SAMPLE
TM = 768

COL_XBC = 0
COL_K = CONV_DIM
COL_V = COL_K + 256
COL_DT = COL_V + 256
COL_Z = 4096
COL_Q = COL_Z + D_INNER
COL_QM = COL_Q + D_MODEL
COL_GATE = COL_QM + D_MODEL
IN_PAD = COL_GATE + N_BRANCH * D_MODEL

NEG_INF = float("-inf")
NEVER = 2.0


def _cparams(sem):
    return pltpu.CompilerParams(dimension_semantics=sem, vmem_limit_bytes=VMEM_LIMIT)


def _nt_dot(a, b):
    return lax.dot_general(a, b, (((1,), (1,)), ((), ())), preferred_element_type=F32)


def _tn_dot(a, b):
    return lax.dot_general(a, b, (((0,), (0,)), ((), ())), preferred_element_type=F32)


def _rms(y, gain):
    ms = jnp.mean(y * y, axis=-1, keepdims=True)
    return y * lax.rsqrt(ms + EPS) * gain


def _norm_matmul_kernel(x_ref, g_ref, w_ref, gn_ref, o_ref, h_ref, *, norm_lo, norm_hi):
    n = pl.program_id(1)

    @pl.when(n == 0)
    def _():
        h_ref[...] = _rms(x_ref[...], g_ref[...]).astype(BF16)

    y = jnp.dot(h_ref[...], w_ref[...], preferred_element_type=F32)
    is_norm = jnp.logical_and(n >= norm_lo, n < norm_hi)

    @pl.when(is_norm)
    def _():
        for c in range(y.shape[1] // MEM_HEAD_DIM):
            sl = slice(c * MEM_HEAD_DIM, (c + 1) * MEM_HEAD_DIM)
            o_ref[:, sl] = _rms(y[:, sl], gn_ref[...])

    @pl.when(jnp.logical_not(is_norm))
    def _():
        o_ref[...] = y


def _norm_matmul(x, g, w, gn, layer, *, tm, tn, norm_lo, norm_hi):
    m, k = x.shape
    n = w.shape[2]
    return pl.pallas_call(
        functools.partial(_norm_matmul_kernel, norm_lo=norm_lo, norm_hi=norm_hi),
        grid=(m // tm, n // tn),
        in_specs=[
            pl.BlockSpec((tm, k), lambda i, j: (i, 0)),
            pl.BlockSpec((1, k), lambda i, j: (0, 0)),
            pl.BlockSpec((None, k, tn), lambda i, j: (layer, 0, j)),
            pl.BlockSpec((1, MEM_HEAD_DIM), lambda i, j: (0, 0)),
        ],
        out_specs=pl.BlockSpec((tm, tn), lambda i, j: (i, j)),
        out_shape=jax.ShapeDtypeStruct((m, n), F32),
        scratch_shapes=[pltpu.VMEM((tm, k), BF16)],
        compiler_params=_cparams(("parallel", "arbitrary")),
        name="norm_matmul",
    )(x, g, w, gn)


def _ssd_kernel(*refs, q, nc, has_init):
    if has_init:
        (xbc_ref, z_ref, dt_ref, cw_ref, cb_ref, dtb_ref, alog_ref, dskip_ref, gn_ref, conv0_ref, ssm0_ref,
         _, y_ref, ssm_out_ref, tail_ref, ext_ref, s_ref) = refs
    else:
        (xbc_ref, z_ref, dt_ref, cw_ref, cb_ref, dtb_ref, alog_ref, dskip_ref, gn_ref,
         y_ref, ssm_out_ref, tail_ref, ext_ref, s_ref) = refs
    c = pl.program_id(1)

    @pl.when(c == 0)
    def _():
        if has_init:
            ext_ref[0:SUBLANES, :] = conv0_ref[0]
            s_ref[...] = ssm0_ref[0]
        else:
            ext_ref[0:SUBLANES, :] = jnp.zeros((SUBLANES, CONV_DIM), F32)
            s_ref[...] = jnp.zeros_like(s_ref)

    ext_ref[SUBLANES:SUBLANES + q, :] = xbc_ref[...]
    w = cw_ref[...]
    conv = (cb_ref[...]
            + w[3:4] * ext_ref[SUBLANES:SUBLANES + q, :]
            + w[2:3] * ext_ref[SUBLANES - 1:SUBLANES - 1 + q, :]
            + w[1:2] * ext_ref[SUBLANES - 2:SUBLANES - 2 + q, :]
            + w[0:1] * ext_ref[SUBLANES - 3:SUBLANES - 3 + q, :])
    tail = ext_ref[q:q + SUBLANES, :]
    ext_ref[0:SUBLANES, :] = tail
    xc = conv * jax.nn.sigmoid(conv)

    gn_w = SSD_GROUPS * D_STATE
    bm = xc[:, D_INNER:D_INNER + gn_w].astype(BF16)
    cm = xc[:, D_INNER + gn_w:].astype(BF16)

    dtr = dt_ref[...] + dtb_ref[...]
    dt = jnp.maximum(dtr, 0.0) + jnp.log1p(jnp.exp(-jnp.abs(dtr)))
    a_neg = -jnp.exp(alog_ref[...])
    da = dt * a_neg
    row = lax.broadcasted_iota(jnp.int32, (q, q), 0)
    col = lax.broadcasted_iota(jnp.int32, (q, q), 1)
    causal = row >= col
    tri = causal.astype(F32)
    cum = jnp.dot(tri, da, preferred_element_type=F32, precision=lax.Precision.HIGHEST)
    if q < LANES:
        cum_sq = jnp.concatenate([cum, jnp.zeros((LANES - q, LANES), F32)], axis=0)
    else:
        cum_sq = cum
    cum_t = cum_sq.T[:, :q]
    ecum = jnp.exp(cum)
    cl = cum[q - 1:q, :]
    wend = jnp.exp(cl - cum)
    ecl = jnp.exp(cl)

    lo = lax.broadcasted_iota(jnp.int32, (q, LANES), 1) < SSD_HEAD_DIM
    row_lo = lax.broadcasted_iota(jnp.int32, (LANES, LANES), 0) < SSD_HEAD_DIM
    pairs_per_group = SSD_HEADS // SSD_GROUPS // 2

    for g in range(SSD_GROUPS):
        bg = bm[:, g * D_STATE:(g + 1) * D_STATE]
        cg = cm[:, g * D_STATE:(g + 1) * D_STATE]
        cb = _nt_dot(cg, bg)
        y_parts = []
        for pp in range(pairs_per_group):
            hp = g * pairs_per_group + pp
            ha, hb = 2 * hp, 2 * hp + 1
            sl = slice(hp * LANES, (hp + 1) * LANES)
            xp = xc[:, sl]
            xdt = xp * jnp.where(lo, dt[:, ha:ha + 1], dt[:, hb:hb + 1])
            xdt_b = xdt.astype(BF16)
            seg_a = cum[:, ha:ha + 1] - cum_t[ha:ha + 1, :]
            seg_b = cum[:, hb:hb + 1] - cum_t[hb:hb + 1, :]
            m_a = (cb * jnp.exp(jnp.where(causal, seg_a, NEG_INF))).astype(BF16)
            m_b = (cb * jnp.exp(jnp.where(causal, seg_b, NEG_INF))).astype(BF16)
            y_a = jnp.dot(m_a, xdt_b, preferred_element_type=F32)
            y_b = jnp.dot(m_b, xdt_b, preferred_element_type=F32)
            y = jnp.where(lo, y_a, y_b)
            sp = s_ref[hp]
            cs = _nt_dot(cg, sp.astype(BF16))
            y = y + cs * jnp.where(lo, ecum[:, ha:ha + 1], ecum[:, hb:hb + 1])
            xw = (xdt * jnp.where(lo, wend[:, ha:ha + 1], wend[:, hb:hb + 1])).astype(BF16)
            upd = _tn_dot(xw, bg)
            dec = jnp.where(row_lo, ecl[:, ha:ha + 1], ecl[:, hb:hb + 1])
            s_ref[hp] = sp * dec + upd
            y = y + dskip_ref[:, sl] * xp
            zz = z_ref[:, sl]
            y = y * (zz * jax.nn.sigmoid(zz))
            y_parts.append(y)
        yg = jnp.concatenate(y_parts, axis=1)
        gw = D_INNER // SSD_GROUPS
        y_ref[:, g * gw:(g + 1) * gw] = _rms(yg, gn_ref[:, g * gw:(g + 1) * gw])

    @pl.when(c == nc - 1)
    def _():
        ssm_out_ref[0] = s_ref[...]
        tail_ref[0] = tail


def _ssd(y_in, cw, cb, dtb, alog, dskip, gn, conv0, ssm0, prev, layer, *, nb, q, nc, row0):
    rb = row0 // q
    half = SSD_HEADS // 2
    in_specs = [
        pl.BlockSpec((q, CONV_DIM), lambda b, c: (rb + b * nc + c, COL_XBC // CONV_DIM)),
        pl.BlockSpec((q, D_INNER), lambda b, c: (rb + b * nc + c, COL_Z // D_INNER)),
        pl.BlockSpec((q, LANES), lambda b, c: (rb + b * nc + c, COL_DT // LANES)),
        pl.BlockSpec((CONV_W, CONV_DIM), lambda b, c: (0, 0)),
        pl.BlockSpec((1, CONV_DIM), lambda b, c: (0, 0)),
        pl.BlockSpec((1, LANES), lambda b, c: (0, 0)),
        pl.BlockSpec((1, LANES), lambda b, c: (0, 0)),
        pl.BlockSpec((1, D_INNER), lambda b, c: (0, 0)),
        pl.BlockSpec((1, D_INNER), lambda b, c: (0, 0)),
    ]
    args = [y_in, y_in, y_in, cw, cb, dtb, alog, dskip, gn]
    aliases = {}
    has_init = conv0 is not None
    if has_init:
        in_specs += [
            pl.BlockSpec((None, 1, SUBLANES, CONV_DIM), lambda b, c: (layer, b, 0, 0)),
            pl.BlockSpec((None, 1, half, LANES, D_STATE), lambda b, c: (layer, b, 0, 0, 0)),
            pl.BlockSpec(memory_space=pl.ANY),
        ]
        args += [conv0, ssm0, prev]
        aliases = {len(args) - 1: 0}
    return pl.pallas_call(
        functools.partial(_ssd_kernel, q=q, nc=nc, has_init=has_init),
        grid=(nb, nc),
        in_specs=in_specs,
        out_specs=[
            pl.BlockSpec((q, D_INNER), lambda b, c: (rb + b * nc + c, 0)),
            pl.BlockSpec((1, half, LANES, D_STATE), lambda b, c: (b, 0, 0, 0)),
            pl.BlockSpec((1, SUBLANES, CONV_DIM), lambda b, c: (b, 0, 0)),
        ],
        out_shape=[
            jax.ShapeDtypeStruct((T_ALL, D_INNER), F32),
            jax.ShapeDtypeStruct((nb, half, LANES, D_STATE), F32),
            jax.ShapeDtypeStruct((nb, SUBLANES, CONV_DIM), F32),
        ],
        scratch_shapes=[
            pltpu.VMEM((q + SUBLANES, CONV_DIM), F32),
            pltpu.VMEM((half, LANES, D_STATE), F32),
        ],
        input_output_aliases=aliases,
        compiler_params=_cparams(("parallel", "arbitrary")),
        name="ssd_q%d" % q,
    )(*args)


def _half_rmsnorm(x, gain, lo):
    sq = x * x
    ms_lo = jnp.sum(jnp.where(lo, sq, 0.0), axis=-1, keepdims=True) * (1.0 / SWA_HEAD_DIM)
    ms_hi = jnp.sum(jnp.where(lo, 0.0, sq), axis=-1, keepdims=True) * (1.0 / SWA_HEAD_DIM)
    return x * jnp.where(lo, lax.rsqrt(ms_lo + EPS), lax.rsqrt(ms_hi + EPS)) * gain


def _both_halves(x, lo, take_lo):
    r = pltpu.roll(x, SWA_HEAD_DIM, axis=1)
    return jnp.where(lo, x, r) if take_lo else jnp.where(lo, r, x)


def _pad_rows(x, rows):
    if x.shape[0] == rows:
        return x
    return jnp.concatenate([x, jnp.zeros((rows - x.shape[0], x.shape[1]), x.dtype)], axis=0)


def _swa_kernel(*refs, lq, cached):
    if cached:
        sink_ref, slope_ref, q_ref, kc_ref, vc_ref, kp_ref, vp_ref, gq_ref, gk_ref, _, o_ref, kn_ref = refs
        kp, vp = kp_ref[0], vp_ref[0]
    else:
        sink_ref, slope_ref, q_ref, kc_ref, vc_ref, kp_ref, vp_ref, gq_ref, gk_ref, o_ref, kn_ref = refs
        kp, vp = kp_ref[...], vp_ref[...]
    gq_heads = SWA_HEADS // SWA_KV_HEADS
    rows = gq_heads * lq
    lo_q = lax.broadcasted_iota(jnp.int32, (lq, LANES), 1) < SWA_HEAD_DIM
    lo_w = lax.broadcasted_iota(jnp.int32, (WINDOW, LANES), 1) < SWA_HEAD_DIM
    gq = gq_ref[...]
    gk = gk_ref[...]

    kc = kc_ref[...]
    vc = vc_ref[...]
    kn_slabs, kp_slabs = [], []
    for s in range(2):
        sl = slice(s * LANES, (s + 1) * LANES)
        kn_s = _half_rmsnorm(kc[:, sl], gk, lo_q)
        kn_ref[:, sl] = kn_s
        kn_slabs.append(kn_s)
        kp_slabs.append(kp[:, sl] if cached else _half_rmsnorm(kp[:, sl], gk, lo_w))

    r_idx = lax.broadcasted_iota(jnp.int32, (rows, WINDOW), 0)
    j_idx = lax.broadcasted_iota(jnp.int32, (rows, WINDOW), 1)
    t_idx = r_idx % lq
    mask_prev = j_idx >= t_idx
    if not cached:
        mask_prev = jnp.logical_and(mask_prev, pl.program_id(1) > 0)
    mask_cur = j_idx <= t_idx
    dist_prev = (t_idx + WINDOW - j_idx).astype(F32)
    dist_cur = (t_idx - j_idx).astype(F32)
    head_of_row = lax.broadcasted_iota(jnp.int32, (rows, 1), 0) // lq
    scale = SWA_HEAD_DIM ** -0.5

    for kvh in range(SWA_KV_HEADS):
        slab, take_lo = kvh // 2, (kvh % 2 == 0)
        sl = slice(slab * LANES, (slab + 1) * LANES)
        k_cur = _pad_rows(_both_halves(kn_slabs[slab], lo_q, take_lo), WINDOW).astype(BF16)
        v_cur = _pad_rows(_both_halves(vc[:, sl], lo_q, take_lo), WINDOW).astype(BF16)
        k_prev = _both_halves(kp_slabs[slab], lo_w, take_lo).astype(BF16)
        v_prev = _both_halves(vp[:, sl], lo_w, take_lo).astype(BF16)

        q_parts = []
        for p in range(gq_heads // 2):
            c0 = kvh * gq_heads * SWA_HEAD_DIM + p * LANES
            qn = _half_rmsnorm(q_ref[:, c0:c0 + LANES], gq, lo_q)
            q_parts.append(jnp.where(lo_q, qn, 0.0))
            q_parts.append(jnp.where(lo_q, 0.0, qn))
        qs = jnp.concatenate(q_parts, axis=0).astype(BF16)

        sink = jnp.zeros((rows, 1), F32)
        slope = jnp.zeros((rows, 1), F32)
        for gi in range(gq_heads):
            sink = jnp.where(head_of_row == gi, sink_ref[kvh * gq_heads + gi], sink)
            slope = jnp.where(head_of_row == gi, slope_ref[kvh * gq_heads + gi], slope)

        s_prev = _nt_dot(qs, k_prev) * scale - slope * dist_prev
        s_cur = _nt_dot(qs, k_cur) * scale - slope * dist_cur
        s_prev = jnp.where(mask_prev, s_prev, NEG_INF)
        s_cur = jnp.where(mask_cur, s_cur, NEG_INF)
        m = jnp.maximum(jnp.maximum(jnp.max(s_prev, axis=-1, keepdims=True),
                                    jnp.max(s_cur, axis=-1, keepdims=True)), sink)
        p_prev = jnp.exp(s_prev - m)
        p_cur = jnp.exp(s_cur - m)
        den = (jnp.sum(p_prev, axis=-1, keepdims=True) + jnp.sum(p_cur, axis=-1, keepdims=True)
               + jnp.exp(sink - m))
        o = (jnp.dot(p_prev.astype(BF16), v_prev, preferred_element_type=F32)
             + jnp.dot(p_cur.astype(BF16), v_cur, preferred_element_type=F32)) / den
        for p in range(gq_heads // 2):
            c0 = kvh * gq_heads * SWA_HEAD_DIM + p * LANES
            o_lo = o[(2 * p) * lq:(2 * p + 1) * lq]
            o_hi = o[(2 * p + 1) * lq:(2 * p + 2) * lq]
            o_ref[:, c0:c0 + LANES] = jnp.where(lo_q, o_lo, o_hi)


def _swa(y_in, sinks, slopes, gq2, gk2, win_k, win_v, prev, layer, *, nb, lq, nq, row0):
    rb = row0 // lq
    kw = SWA_KV_HEADS * SWA_HEAD_DIM
    cached = win_k is not None
    in_specs = [
        pl.BlockSpec(memory_space=pltpu.SMEM),
        pl.BlockSpec(memory_space=pltpu.SMEM),
        pl.BlockSpec((lq, D_MODEL), lambda b, i: (rb + b * nq + i, COL_Q // D_MODEL)),
        pl.BlockSpec((lq, kw), lambda b, i: (rb + b * nq + i, COL_K // kw)),
        pl.BlockSpec((lq, kw), lambda b, i: (rb + b * nq + i, COL_V // kw)),
    ]
    args = [sinks, slopes, y_in, y_in, y_in]
    if cached:
        in_specs += [
            pl.BlockSpec((None, 1, WINDOW, kw), lambda b, i: (layer, b, 0, 0)),
            pl.BlockSpec((None, 1, WINDOW, kw), lambda b, i: (layer, b, 0, 0)),
        ]
        args += [win_k, win_v]
    else:
        in_specs += [
            pl.BlockSpec((lq, kw), lambda b, i: (rb + b * nq + jnp.maximum(i - 1, 0), COL_K // kw)),
            pl.BlockSpec((lq, kw), lambda b, i: (rb + b * nq + jnp.maximum(i - 1, 0), COL_V // kw)),
        ]
        args += [y_in, y_in]
    in_specs += [pl.BlockSpec((1, LANES), lambda b, i: (0, 0)), pl.BlockSpec((1, LANES), lambda b, i: (0, 0))]
    args += [gq2, gk2]
    aliases = {}
    if cached:
        in_specs.append(pl.BlockSpec(memory_space=pl.ANY))
        args.append(prev)
        aliases = {len(args) - 1: 0}
    return pl.pallas_call(
        functools.partial(_swa_kernel, lq=lq, cached=cached),
        grid=(nb, nq),
        in_specs=in_specs,
        out_specs=[
            pl.BlockSpec((lq, D_MODEL), lambda b, i: (rb + b * nq + i, 0)),
            pl.BlockSpec((lq, kw), lambda b, i: (b * nq + i, 0)),
        ],
        out_shape=[
            jax.ShapeDtypeStruct((T_ALL, D_MODEL), F32),
            jax.ShapeDtypeStruct((nb * nq * lq, kw), F32),
        ],
        input_output_aliases=aliases,
        compiler_params=_cparams(("parallel", "arbitrary")),
        name="swa_lq%d" % lq,
    )(*args)


def _mem_attn_kernel(*refs, cached):
    if cached:
        q_ref, k_ref, v_ref, _, o_ref = refs
    else:
        q_ref, k_ref, v_ref, o_ref = refs
    scale = MEM_HEAD_DIM ** -0.5
    for h in range(MEM_HEADS):
        sl = slice(h * MEM_HEAD_DIM, (h + 1) * MEM_HEAD_DIM)
        qh = q_ref[:, sl].astype(BF16)
        if cached:
            kh = k_ref[0, :, sl].astype(BF16)
            vh = v_ref[0, :, sl].astype(BF16)
        else:
            kh = k_ref[:, sl].astype(BF16)
            vh = v_ref[:, sl].astype(BF16)
        s = _nt_dot(qh, kh) * scale
        m = jnp.max(s, axis=-1, keepdims=True)
        p = jnp.exp(s - m)
        den = jnp.sum(p, axis=-1, keepdims=True)
        o_ref[:, sl] = jnp.dot(p.astype(BF16), vh, preferred_element_type=F32) / den


def _mem_attn(y_in, mem_k, mem_v, prev, layer, *, nb, tq, nq, row0):
    rb = row0 // tq
    cached = prev is not None
    in_specs = [pl.BlockSpec((tq, D_MODEL), lambda b, i: (rb + b * nq + i, COL_QM // D_MODEL))]
    args = [y_in, mem_k, mem_v]
    aliases = {}
    if cached:
        in_specs += [
            pl.BlockSpec((None, 1, N_MEM, D_MODEL), lambda b, i: (layer, b, 0, 0)),
            pl.BlockSpec((None, 1, N_MEM, D_MODEL), lambda b, i: (layer, b, 0, 0)),
            pl.BlockSpec(memory_space=pl.ANY),
        ]
        args.append(prev)
        aliases = {len(args) - 1: 0}
    else:
        in_specs += [
            pl.BlockSpec((N_MEM, D_MODEL), lambda b, i: (b, 0)),
            pl.BlockSpec((N_MEM, D_MODEL), lambda b, i: (b, 1)),
        ]
    return pl.pallas_call(
        functools.partial(_mem_attn_kernel, cached=cached),
        grid=(nb, nq),
        in_specs=in_specs,
        out_specs=pl.BlockSpec((tq, D_MODEL), lambda b, i: (rb + b * nq + i, 0)),
        out_shape=jax.ShapeDtypeStruct((T_ALL, D_MODEL), F32),
        input_output_aliases=aliases,
        compiler_params=_cparams(("parallel", "arbitrary")),
        name="mem_attn_tq%d" % tq,
    )(*args)


def _merge_kernel(a0_ref, a1_ref, a2_ref, w0_ref, w1_ref, w2_ref, g0_ref, g1_ref, g2_ref,
                  b0_ref, b1_ref, b2_ref, o_ref, ab_ref):
    @pl.when(pl.program_id(1) == 0)
    def _():
        for k, a_ref in enumerate((a0_ref, a1_ref, a2_ref)):
            ab_ref[k] = a_ref[...].astype(BF16)

    acc = None
    for k, (w_ref, g_ref, b_ref) in enumerate(((w0_ref, g0_ref, b0_ref), (w1_ref, g1_ref, b1_ref),
                                               (w2_ref, g2_ref, b2_ref))):
        br = jnp.dot(ab_ref[k], w_ref[...], preferred_element_type=F32)
        t = jax.nn.sigmoid(g_ref[...] + b_ref[...]) * br
        acc = t if acc is None else acc + t
    o_ref[...] = acc.astype(BF16)


def _merge(a0, a1, a2, w0, w1, w2, y_in, b_gate, layer, *, tm, tn):
    gb = COL_GATE // tn
    nbk = D_MODEL // tn
    a_spec = pl.BlockSpec((tm, D_MODEL), lambda i, j: (i, 0), pipeline_mode=pl.Buffered(1))
    w_spec = pl.BlockSpec((None, D_MODEL, tn), lambda i, j: (layer, 0, j))
    return pl.pallas_call(
        _merge_kernel,
        grid=(T_ALL // tm, nbk),
        in_specs=[a_spec, a_spec, a_spec, w_spec, w_spec, w_spec,
                  pl.BlockSpec((tm, tn), lambda i, j: (i, gb + j)),
                  pl.BlockSpec((tm, tn), lambda i, j: (i, gb + nbk + j)),
                  pl.BlockSpec((tm, tn), lambda i, j: (i, gb + 2 * nbk + j)),
                  pl.BlockSpec((1, tn), lambda i, j: (0, j)),
                  pl.BlockSpec((1, tn), lambda i, j: (0, nbk + j)),
                  pl.BlockSpec((1, tn), lambda i, j: (0, 2 * nbk + j))],
        out_specs=pl.BlockSpec((tm, tn), lambda i, j: (i, j)),
        out_shape=jax.ShapeDtypeStruct((T_ALL, D_MODEL), BF16),
        scratch_shapes=[pltpu.VMEM((N_BRANCH, tm, D_MODEL), BF16)],
        compiler_params=_cparams(("parallel", "arbitrary")),
        name="gated_merge",
    )(a0, a1, a2, w0, w1, w2, y_in, y_in, y_in, b_gate, b_gate, b_gate)


def _matmul_res_kernel(a_ref, w_ref, r_ref, o_ref):
    o_ref[...] = r_ref[...] + jnp.dot(a_ref[...], w_ref[...], preferred_element_type=F32)


def _matmul_res(a, w, r, layer, *, tn):
    m, k = a.shape
    n = w.shape[2]
    return pl.pallas_call(
        _matmul_res_kernel,
        grid=(m // TM, n // tn),
        in_specs=[pl.BlockSpec((TM, k), lambda i, j: (i, 0)),
                  pl.BlockSpec((None, k, tn), lambda i, j: (layer, 0, j)),
                  pl.BlockSpec((TM, tn), lambda i, j: (i, j))],
        out_specs=pl.BlockSpec((TM, tn), lambda i, j: (i, j)),
        out_shape=jax.ShapeDtypeStruct((m, n), F32),
        compiler_params=_cparams(("parallel", "arbitrary")),
        name="matmul_residual",
    )(a, w, r)


def _top_values(s, k):
    vals = []
    cur = s
    for r in range(k):
        mx = jnp.max(cur, axis=0, keepdims=True)
        vals.append(mx)
        if r + 1 < k:
            cur = jnp.where(cur == mx, NEG_INF, cur)
    return vals


def _router_kernel(x_ref, g_ref, wqt_ref, keys_ref, h2_ref, e0_ref, th_ref, e1_ref):
    h2 = _rms(x_ref[...], g_ref[...]).astype(BF16)
    h2_ref[...] = h2
    qt = _nt_dot(wqt_ref[...], h2)
    for h in range(PEER_HEADS):
        s0 = jnp.dot(keys_ref[2 * h], qt[(2 * h) * PEER_HALF:(2 * h + 1) * PEER_HALF, :].astype(BF16),
                     preferred_element_type=F32)
        s1 = jnp.dot(keys_ref[2 * h + 1], qt[(2 * h + 1) * PEER_HALF:(2 * h + 2) * PEER_HALF, :].astype(BF16),
                     preferred_element_type=F32)
        top0 = _top_values(s0, PEER_TOPK)
        top1 = _top_values(s1, PEER_TOPK)
        a1 = jnp.concatenate(top1, axis=0)
        cand = jnp.concatenate([top0[r] + a1 for r in range(PEER_TOPK)], axis=0)
        best = _top_values(cand, PEER_TOPK)
        m = best[0]
        tau = best[PEER_TOPK - 1]
        z = jnp.zeros_like(m)
        for r in range(PEER_TOPK):
            z = z + jnp.exp(best[r] - m)
        e1_sorted = jnp.exp(a1 - top1[0])
        th = jnp.full(s0.shape, NEVER, F32)
        for r in range(PEER_TOPK):
            th_r = jnp.min(jnp.where((top0[r] + a1) >= tau, e1_sorted, NEVER), axis=0, keepdims=True)
            th = jnp.where(s0 == top0[r], th_r, th)
        e0_ref[h] = 0.5 * jnp.exp(s0 - top0[0]) / z
        th_ref[h] = th
        e1_ref[h] = jnp.exp(s1 - top1[0])


def _router(x, g, wqt, keys, layer, *, tmr):
    t = x.shape[0]
    tok = pl.BlockSpec((PEER_HEADS, N_KEYS, tmr), lambda i: (0, 0, i))
    per_key = jax.ShapeDtypeStruct((PEER_HEADS, N_KEYS, t), F32)
    return pl.pallas_call(
        _router_kernel,
        grid=(t // tmr,),
        in_specs=[pl.BlockSpec((tmr, D_MODEL), lambda i: (i, 0)),
                  pl.BlockSpec((1, D_MODEL), lambda i: (0, 0)),
                  pl.BlockSpec((None, PEER_HEADS * PEER_QUERY_DIM, D_MODEL), lambda i: (layer, 0, 0)),
                  pl.BlockSpec((None, 2 * PEER_HEADS, N_KEYS, PEER_HALF), lambda i: (layer, 0, 0, 0))],
        out_specs=[pl.BlockSpec((tmr, D_MODEL), lambda i: (i, 0)), tok, tok, tok],
        out_shape=[jax.ShapeDtypeStruct((t, D_MODEL), BF16), per_key, per_key, per_key],
        compiler_params=_cparams(("parallel",)),
        name="peer_router",
    )(x, g, wqt, keys)


def _peer_dense_kernel(h2_ref, u_ref, vt_ref, e0_ref, th_ref, e1_ref, x_ref, o_ref,
                       acc_ref, at_ref, wa_ref, wb_ref, h2t_ref, *, te, ne, tm):
    e = pl.program_id(1)
    per_step = te // N_KEYS
    assert 2 * per_step == SUBLANES

    @pl.when(e == 0)
    def _():
        acc_ref[...] = jnp.zeros_like(acc_ref)
        wb_ref[...] = jnp.zeros_like(wb_ref)
        h2t_ref[...] = h2_ref[...].T

    chunk = 2 * LANES
    n_chunks = tm // chunk
    vrows = D_MODEL // (2 * per_step)

    def step(w_new_ref, w_old_ref, parity):
        def score_piece(c, ii):
            rows = slice(ii * N_KEYS, (ii + 1) * N_KEYS)
            cols = slice(c * chunk, (c + 1) * chunk)
            at_ref[rows, cols] = jnp.dot(u_ref[rows, :], h2t_ref[:, cols], preferred_element_type=F32)

        def value_piece(c, m):
            rows = slice(m * vrows, (m + 1) * vrows)
            cols = slice(c * chunk, (c + 1) * chunk)
            acc_ref[rows, cols] += jnp.dot(vt_ref[rows, :], w_old_ref[:, cols], preferred_element_type=F32)

        def gate_cell(ii, tg):
            rows = slice(ii * N_KEYS, (ii + 1) * N_KEYS)
            cs = slice(tg * LANES, (tg + 1) * LANES)
            r = parity * per_step + ii
            gate = None
            for h in range(PEER_HEADS):
                e1 = e1_ref[h, :, cs]
                t = jnp.where(e1 >= th_ref[h, r:r + 1, cs], e1 * e0_ref[h, r:r + 1, cs], 0.0)
                gate = t if gate is None else gate + t
            a = at_ref[rows, cs]
            gelu2 = a * (1.0 + lax.erf(a * (2.0 ** -0.5)))
            w_new_ref[rows, cs] = (gate * gelu2).astype(BF16)

        rounds = [(c, ii) for c in range(n_chunks) for ii in range(per_step)]
        score_piece(*rounds[0])
        for k, (c, ii) in enumerate(rounds):
            if k + 1 < len(rounds):
                score_piece(*rounds[k + 1])
            gate_cell(ii, 2 * c)
            value_piece(c, 2 * ii)
            gate_cell(ii, 2 * c + 1)
            value_piece(c, 2 * ii + 1)

    @pl.when(e % 2 == 0)
    def _():
        step(wa_ref, wb_ref, 0)

    @pl.when(e % 2 == 1)
    def _():
        step(wb_ref, wa_ref, 1)

    @pl.when(e == ne)
    def _():
        o_ref[...] = x_ref[...] + acc_ref[...].T


def _peer_dense(h2, u, vt, e0, th, e1, x, layer, *, te, tm):
    t = x.shape[0]
    ne = N_EXPERTS // te
    once = pl.Buffered(1)
    by_first = pl.BlockSpec((PEER_HEADS, SUBLANES, tm), lambda i, e: (0, jnp.minimum(e, ne - 1) // 2, i))
    by_second = pl.BlockSpec((PEER_HEADS, N_KEYS, tm), lambda i, e: (0, 0, i), pipeline_mode=once)
    return pl.pallas_call(
        functools.partial(_peer_dense_kernel, te=te, ne=ne, tm=tm),
        grid=(t // tm, ne + 1),
        in_specs=[pl.BlockSpec((tm, D_MODEL), lambda i, e: (i, 0), pipeline_mode=once),
                  pl.BlockSpec((None, te, D_MODEL), lambda i, e: (layer, jnp.minimum(e, ne - 1), 0)),
                  pl.BlockSpec((None, D_MODEL, te), lambda i, e: (layer, 0, jnp.maximum(e - 1, 0))),
                  by_first, by_first, by_second,
                  pl.BlockSpec((tm, D_MODEL), lambda i, e: (i, 0), pipeline_mode=once)],
        out_specs=pl.BlockSpec((tm, D_MODEL), lambda i, e: (i, 0)),
        out_shape=jax.ShapeDtypeStruct((t, D_MODEL), F32),
        scratch_shapes=[pltpu.VMEM((D_MODEL, tm), F32), pltpu.VMEM((te, tm), F32),
                        pltpu.VMEM((te, tm), BF16), pltpu.VMEM((te, tm), BF16),
                        pltpu.VMEM((D_MODEL, tm), BF16)],
        compiler_params=_cparams(("parallel", "arbitrary")),
        name="peer_dense",
    )(h2, u, vt, e0, th, e1, x)


def _prep_w_in(w):
    z, xbc, dt, q, k, v, qm, gate = jnp.split(
        w, [D_INNER, D_INNER + CONV_DIM, D_INNER + CONV_DIM + SSD_HEADS,
            D_INNER + CONV_DIM + SSD_HEADS + D_MODEL,
            D_INNER + CONV_DIM + SSD_HEADS + D_MODEL + 256,
            D_INNER + CONV_DIM + SSD_HEADS + D_MODEL + 512,
            D_INNER + CONV_DIM + SSD_HEADS + 2 * D_MODEL + 512], axis=2)
    pad = jnp.zeros((DEPTH, D_MODEL, COL_Z - COL_DT - SSD_HEADS), w.dtype)
    return jnp.concatenate([xbc, k, v, dt, pad, z, q, qm, gate], axis=2).astype(BF16)


def _pad_lanes(v, n=LANES):
    return jnp.pad(v, (0, n - v.shape[0])).reshape(1, n)


def kernel(x_prompt, x_sample, cache_win_k, cache_win_v, state_ssm, state_conv, cache_mem_k, cache_mem_v,
           mem_prompt, w_in, b_gate, conv_w, conv_b, dt_bias, a_log, d_skip, g_ssd_norm, g_q, g_k,
           attn_sinks, g_mem, w_mem_kv, g_qm, g_km, w_o_ssm, w_o_swa, w_o_mem, w_out, g_mix, g_ffn,
           w_peer_q, peer_sub_keys, peer_u, peer_v):
    x = jnp.concatenate([x_prompt.reshape(T_PROMPT, D_MODEL), x_sample.reshape(T_SAMPLE, D_MODEL)], axis=0)
    slopes = jnp.exp2(-8.0 * jnp.arange(1, SWA_HEADS + 1, dtype=F32) / SWA_HEADS)
    mem_x = mem_prompt.reshape(BATCH * N_MEM, D_MODEL)
    kw = SWA_KV_HEADS * SWA_HEAD_DIM
    half = SSD_HEADS // 2
    row = lambda v: v.reshape(1, -1)

    w_in_b = _prep_w_in(w_in)
    w_mem_kv_b = w_mem_kv.astype(BF16)
    w_o_ssm_b, w_o_swa_b, w_o_mem_b = w_o_ssm.astype(BF16), w_o_swa.astype(BF16), w_o_mem.astype(BF16)
    w_out_b = w_out.astype(BF16)
    wqt_b = jnp.swapaxes(w_peer_q, 1, 2).astype(BF16)
    keys_b = peer_sub_keys.reshape(DEPTH, 2 * PEER_HEADS, N_KEYS, PEER_HALF).astype(BF16)
    u_b = peer_u.astype(BF16)
    vt_b = jnp.swapaxes(peer_v, 1, 2).astype(BF16)
    conv0 = jnp.pad(state_conv, ((0, 0), (0, 0), (SUBLANES - (CONV_W - 1), 0), (0, 0)))
    ssm0 = state_ssm.reshape(DEPTH, DEC_BATCH, half, LANES, D_STATE)
    wk_c = cache_win_k.reshape(DEPTH, DEC_BATCH, WINDOW, kw)
    wv_c = cache_win_v.reshape(DEPTH, DEC_BATCH, WINDOW, kw)
    mk_c = cache_mem_k.reshape(DEPTH, DEC_BATCH, N_MEM, D_MODEL)
    mv_c = cache_mem_v.reshape(DEPTH, DEC_BATCH, N_MEM, D_MODEL)

    outs = {k: [] for k in ("p_wk", "p_wv", "p_ssm", "p_conv", "p_mk", "p_mv", "s_wk", "s_wv", "s_ssm", "s_conv")}
    for l in range(DEPTH):
        mem_kv = _norm_matmul(mem_x, row(g_mem[l]), w_mem_kv_b, row(g_km[l]), l,
                              tm=512, tn=MEM_HEAD_DIM, norm_lo=0, norm_hi=MEM_HEADS)
        outs["p_mk"].append(mem_kv[:, :D_MODEL].reshape(BATCH, N_MEM, MEM_HEADS, MEM_HEAD_DIM))
        outs["p_mv"].append(mem_kv[:, D_MODEL:].reshape(BATCH, N_MEM, MEM_HEADS, MEM_HEAD_DIM))

        tn_in = 1024
        y = _norm_matmul(x, row(g_mix[l]), w_in_b, row(g_qm[l]), l, tm=TM, tn=tn_in,
                         norm_lo=COL_QM // tn_in, norm_hi=COL_GATE // tn_in)

        cw, cb = conv_w[l], row(conv_b[l])
        dtb, alog = _pad_lanes(dt_bias[l]), _pad_lanes(a_log[l])
        dskip = row(jnp.repeat(d_skip[l], SSD_HEAD_DIM))
        gn = row(g_ssd_norm[l])
        y_ssm, p_ssm, p_tail = _ssd(y, cw, cb, dtb, alog, dskip, gn, None, None, None, l,
                                    nb=BATCH, q=SSD_CHUNK, nc=SEQ // SSD_CHUNK, row0=0)
        y_ssm, s_ssm, s_tail = _ssd(y, cw, cb, dtb, alog, dskip, gn, conv0, ssm0, y_ssm, l,
                                    nb=DEC_BATCH, q=DEC_SEQ, nc=1, row0=T_PROMPT)
        outs["p_ssm"].append(p_ssm.reshape(BATCH, SSD_HEADS, SSD_HEAD_DIM, D_STATE))
        outs["s_ssm"].append(s_ssm.reshape(DEC_BATCH, SSD_HEADS, SSD_HEAD_DIM, D_STATE))
        outs["p_conv"].append(p_tail[:, SUBLANES - (CONV_W - 1):])
        outs["s_conv"].append(s_tail[:, SUBLANES - (CONV_W - 1):])

        gq2 = row(jnp.concatenate([g_q[l], g_q[l]]))
        gk2 = row(jnp.concatenate([g_k[l], g_k[l]]))
        o_swa, kn_p = _swa(y, attn_sinks[l], slopes, gq2, gk2, None, None, None, l,
                           nb=BATCH, lq=WINDOW, nq=SEQ // WINDOW, row0=0)
        o_swa, kn_s = _swa(y, attn_sinks[l], slopes, gq2, gk2, wk_c, wv_c, o_swa, l,
                           nb=DEC_BATCH, lq=DEC_SEQ, nq=1, row0=T_PROMPT)
        v_p = y[:T_PROMPT, COL_V:COL_V + kw].reshape(BATCH, SEQ, kw)
        v_s = y[T_PROMPT:, COL_V:COL_V + kw].reshape(DEC_BATCH, DEC_SEQ, kw)
        kv_shape = (SWA_KV_HEADS, SWA_HEAD_DIM)
        outs["p_wk"].append(kn_p.reshape(BATCH, SEQ, kw)[:, SEQ - WINDOW:].reshape(BATCH, WINDOW, *kv_shape))
        outs["p_wv"].append(v_p[:, SEQ - WINDOW:].reshape(BATCH, WINDOW, *kv_shape))
        outs["s_wk"].append(jnp.concatenate([wk_c[l][:, DEC_SEQ:], kn_s.reshape(DEC_BATCH, DEC_SEQ, kw)], axis=1)
                            .reshape(DEC_BATCH, WINDOW, *kv_shape))
        outs["s_wv"].append(jnp.concatenate([wv_c[l][:, DEC_SEQ:], v_s], axis=1)
                            .reshape(DEC_BATCH, WINDOW, *kv_shape))

        o_mem = _mem_attn(y, mem_kv, mem_kv, None, l, nb=BATCH, tq=512, nq=SEQ // 512, row0=0)
        o_mem = _mem_attn(y, mk_c, mv_c, o_mem, l, nb=DEC_BATCH, tq=DEC_SEQ, nq=1, row0=T_PROMPT)

        merged = _merge(y_ssm, o_swa, o_mem, w_o_ssm_b, w_o_swa_b, w_o_mem_b, y, row(b_gate[l]), l,
                        tm=TM, tn=256)
        x = _matmul_res(merged, w_out_b, x, l, tn=512)

        h2, e0, th, e1 = _router(x, row(g_ffn[l]), wqt_b, keys_b, l, tmr=256)
        x = _peer_dense(h2, u_b, vt_b, e0, th, e1, x, l, te=512, tm=TM)

    yp = x[:T_PROMPT].reshape(BATCH, SEQ, D_MODEL)
    ys = x[T_PROMPT:].reshape(DEC_BATCH, DEC_SEQ, D_MODEL)
    st = lambda k: jnp.stack(outs[k])
    return (yp, ys, st("p_wk"), st("p_wv"), st("p_ssm"), st("p_conv"), st("p_mk"), st("p_mv"),
            st("s_wk"), st("s_wv"), st("s_ssm"), st("s_conv"))
```

```python
import functools

import jax
import jax.numpy as jnp
from jax import lax
from jax.experimental import pallas as pl
from jax.experimental.pallas import tpu as pltpu

F32 = jnp.float32
BF16 = jnp.bfloat16

D_MODEL = 2048
BATCH = 4
SEQ = 2048
DEPTH = 2
DEC_BATCH = 32
DEC_SEQ = 8
D_INNER = D_MODEL
SSD_HEAD_DIM = 64
SSD_HEADS = D_INNER // SSD_HEAD_DIM
SSD_GROUPS = 4
D_STATE = 128
CONV_W = 4
CONV_DIM = D_INNER + 2 * SSD_GROUPS * D_STATE
SSD_CHUNK = 128
SWA_HEAD_DIM = 64
SWA_HEADS = D_MODEL // SWA_HEAD_DIM
SWA_KV_HEADS = SWA_HEADS // 8
WINDOW = 128
N_MEM = 256
MEM_HEADS = 4
MEM_HEAD_DIM = D_MODEL // MEM_HEADS
PEER_HEADS = 8
N_KEYS = 128
N_EXPERTS = N_KEYS * N_KEYS
PEER_QUERY_DIM = 256
PEER_HALF = PEER_QUERY_DIM // 2
PEER_TOPK = 16
N_BRANCH = 3
EPS = 1e-6

LANES = 128
SUBLANES = 8
VMEM_LIMIT = 56 * 1024 * 1024

T_PROMPT = BATCH * SEQ
T_SAMPLE = DEC_BATCH * DEC_SEQ
T_ALL = T_PROMPT + T_SAMPLE
TM = 768

COL_XBC = 0
COL_K = CONV_DIM
COL_V = COL_K + 256
COL_DT = COL_V + 256
COL_Z = 4096
COL_Q = COL_Z + D_INNER
COL_QM = COL_Q + D_MODEL
COL_GATE = COL_QM + D_MODEL
IN_PAD = COL_GATE + N_BRANCH * D_MODEL

NEG_INF = float("-inf")
NEVER = 2.0


def _cparams(sem, flags=None):
    return pltpu.CompilerParams(dimension_semantics=sem, vmem_limit_bytes=VMEM_LIMIT, flags=flags)


def _nt_dot(a, b):
    return lax.dot_general(a, b, (((1,), (1,)), ((), ())), preferred_element_type=F32)


def _tn_dot(a, b):
    return lax.dot_general(a, b, (((0,), (0,)), ((), ())), preferred_element_type=F32)


def _rms(y, gain):
    ms = jnp.mean(y * y, axis=-1, keepdims=True)
    return y * lax.rsqrt(ms + EPS) * gain


def _norm_matmul_kernel(x_ref, g_ref, w_ref, gn_ref, o_ref, h_ref, *, norm_lo, norm_hi):
    n = pl.program_id(1)

    @pl.when(n == 0)
    def _():
        h_ref[...] = _rms(x_ref[...], g_ref[...]).astype(BF16)

    y = jnp.dot(h_ref[...], w_ref[...], preferred_element_type=F32)
    is_norm = jnp.logical_and(n >= norm_lo, n < norm_hi)

    @pl.when(is_norm)
    def _():
        for c in range(y.shape[1] // MEM_HEAD_DIM):
            sl = slice(c * MEM_HEAD_DIM, (c + 1) * MEM_HEAD_DIM)
            o_ref[:, sl] = _rms(y[:, sl], gn_ref[...])

    @pl.when(jnp.logical_not(is_norm))
    def _():
        o_ref[...] = y


def _norm_matmul(x, g, w, gn, layer, *, tm, tn, norm_lo, norm_hi):
    m, k = x.shape
    n = w.shape[2]
    return pl.pallas_call(
        functools.partial(_norm_matmul_kernel, norm_lo=norm_lo, norm_hi=norm_hi),
        grid=(m // tm, n // tn),
        in_specs=[
            pl.BlockSpec((tm, k), lambda i, j: (i, 0)),
            pl.BlockSpec((1, k), lambda i, j: (0, 0)),
            pl.BlockSpec((None, k, tn), lambda i, j: (layer, 0, j)),
            pl.BlockSpec((1, MEM_HEAD_DIM), lambda i, j: (0, 0)),
        ],
        out_specs=pl.BlockSpec((tm, tn), lambda i, j: (i, j)),
        out_shape=jax.ShapeDtypeStruct((m, n), F32),
        scratch_shapes=[pltpu.VMEM((tm, k), BF16)],
        compiler_params=_cparams(("parallel", "arbitrary")),
        name="norm_matmul",
    )(x, g, w, gn)


def _ssd_kernel(*refs, q, nc, has_init):
    if has_init:
        (xbc_ref, z_ref, dt_ref, cw_ref, cb_ref, dtb_ref, alog_ref, dskip_ref, gn_ref, conv0_ref, ssm0_ref,
         _, y_ref, ssm_out_ref, tail_ref, ext_ref, s_ref) = refs
    else:
        (xbc_ref, z_ref, dt_ref, cw_ref, cb_ref, dtb_ref, alog_ref, dskip_ref, gn_ref,
         y_ref, ssm_out_ref, tail_ref, ext_ref, s_ref) = refs
    c = pl.program_id(1)

    @pl.when(c == 0)
    def _():
        if has_init:
            ext_ref[0:SUBLANES, :] = conv0_ref[0]
            s_ref[...] = ssm0_ref[0]
        else:
            ext_ref[0:SUBLANES, :] = jnp.zeros((SUBLANES, CONV_DIM), F32)
            s_ref[...] = jnp.zeros_like(s_ref)

    ext_ref[SUBLANES:SUBLANES + q, :] = xbc_ref[...]
    w = cw_ref[...]
    conv = (cb_ref[...]
            + w[3:4] * ext_ref[SUBLANES:SUBLANES + q, :]
            + w[2:3] * ext_ref[SUBLANES - 1:SUBLANES - 1 + q, :]
            + w[1:2] * ext_ref[SUBLANES - 2:SUBLANES - 2 + q, :]
            + w[0:1] * ext_ref[SUBLANES - 3:SUBLANES - 3 + q, :])
    tail = ext_ref[q:q + SUBLANES, :]
    ext_ref[0:SUBLANES, :] = tail
    xc = conv * jax.nn.sigmoid(conv)

    gn_w = SSD_GROUPS * D_STATE
    bm = xc[:, D_INNER:D_INNER + gn_w].astype(BF16)
    cm = xc[:, D_INNER + gn_w:].astype(BF16)

    dtr = dt_ref[...] + dtb_ref[...]
    dt = jnp.maximum(dtr, 0.0) + jnp.log1p(jnp.exp(-jnp.abs(dtr)))
    a_neg = -jnp.exp(alog_ref[...])
    da = dt * a_neg
    row = lax.broadcasted_iota(jnp.int32, (q, q), 0)
    col = lax.broadcasted_iota(jnp.int32, (q, q), 1)
    causal = row >= col
    tri = causal.astype(F32)
    cum = jnp.dot(tri, da, preferred_element_type=F32, precision=lax.Precision.HIGHEST)
    if q < LANES:
        cum_sq = jnp.concatenate([cum, jnp.zeros((LANES - q, LANES), F32)], axis=0)
    else:
        cum_sq = cum
    cum_t = cum_sq.T[:, :q]
    ecum = jnp.exp(cum)
    cl = cum[q - 1:q, :]
    wend = jnp.exp(cl - cum)
    ecl = jnp.exp(cl)

    lo = lax.broadcasted_iota(jnp.int32, (q, LANES), 1) < SSD_HEAD_DIM
    row_lo = lax.broadcasted_iota(jnp.int32, (LANES, LANES), 0) < SSD_HEAD_DIM
    pairs_per_group = SSD_HEADS // SSD_GROUPS // 2

    for g in range(SSD_GROUPS):
        bg = bm[:, g * D_STATE:(g + 1) * D_STATE]
        cg = cm[:, g * D_STATE:(g + 1) * D_STATE]
        cb = _nt_dot(cg, bg)
        y_parts = []
        for pp in range(pairs_per_group):
            hp = g * pairs_per_group + pp
            ha, hb = 2 * hp, 2 * hp + 1
            sl = slice(hp * LANES, (hp + 1) * LANES)
            xp = xc[:, sl]
            xdt = xp * jnp.where(lo, dt[:, ha:ha + 1], dt[:, hb:hb + 1])
            xdt_b = xdt.astype(BF16)
            seg_a = cum[:, ha:ha + 1] - cum_t[ha:ha + 1, :]
            seg_b = cum[:, hb:hb + 1] - cum_t[hb:hb + 1, :]
            m_a = (cb * jnp.exp(jnp.where(causal, seg_a, NEG_INF))).astype(BF16)
            m_b = (cb * jnp.exp(jnp.where(causal, seg_b, NEG_INF))).astype(BF16)
            y_a = jnp.dot(m_a, xdt_b, preferred_element_type=F32)
            y_b = jnp.dot(m_b, xdt_b, preferred_element_type=F32)
            y = jnp.where(lo, y_a, y_b)
            sp = s_ref[hp]
            cs = _nt_dot(cg, sp.astype(BF16))
            y = y + cs * jnp.where(lo, ecum[:, ha:ha + 1], ecum[:, hb:hb + 1])
            xw = (xdt * jnp.where(lo, wend[:, ha:ha + 1], wend[:, hb:hb + 1])).astype(BF16)
            upd = _tn_dot(xw, bg)
            dec = jnp.where(row_lo, ecl[:, ha:ha + 1], ecl[:, hb:hb + 1])
            s_ref[hp] = sp * dec + upd
            y = y + dskip_ref[:, sl] * xp
            zz = z_ref[:, sl]
            y = y * (zz * jax.nn.sigmoid(zz))
            y_parts.append(y)
        yg = jnp.concatenate(y_parts, axis=1)
        gw = D_INNER // SSD_GROUPS
        y_ref[:, g * gw:(g + 1) * gw] = _rms(yg, gn_ref[:, g * gw:(g + 1) * gw])

    @pl.when(c == nc - 1)
    def _():
        ssm_out_ref[0] = s_ref[...]
        tail_ref[0] = tail


def _ssd(y_in, cw, cb, dtb, alog, dskip, gn, conv0, ssm0, prev, layer, *, nb, q, nc, row0):
    rb = row0 // q
    half = SSD_HEADS // 2
    in_specs = [
        pl.BlockSpec((q, CONV_DIM), lambda b, c: (rb + b * nc + c, COL_XBC // CONV_DIM)),
        pl.BlockSpec((q, D_INNER), lambda b, c: (rb + b * nc + c, COL_Z // D_INNER)),
        pl.BlockSpec((q, LANES), lambda b, c: (rb + b * nc + c, COL_DT // LANES)),
        pl.BlockSpec((CONV_W, CONV_DIM), lambda b, c: (0, 0)),
        pl.BlockSpec((1, CONV_DIM), lambda b, c: (0, 0)),
        pl.BlockSpec((1, LANES), lambda b, c: (0, 0)),
        pl.BlockSpec((1, LANES), lambda b, c: (0, 0)),
        pl.BlockSpec((1, D_INNER), lambda b, c: (0, 0)),
        pl.BlockSpec((1, D_INNER), lambda b, c: (0, 0)),
    ]
    args = [y_in, y_in, y_in, cw, cb, dtb, alog, dskip, gn]
    aliases = {}
    has_init = conv0 is not None
    if has_init:
        in_specs += [
            pl.BlockSpec((None, 1, SUBLANES, CONV_DIM), lambda b, c: (layer, b, 0, 0)),
            pl.BlockSpec((None, 1, half, LANES, D_STATE), lambda b, c: (layer, b, 0, 0, 0)),
            pl.BlockSpec(memory_space=pl.ANY),
        ]
        args += [conv0, ssm0, prev]
        aliases = {len(args) - 1: 0}
    return pl.pallas_call(
        functools.partial(_ssd_kernel, q=q, nc=nc, has_init=has_init),
        grid=(nb, nc),
        in_specs=in_specs,
        out_specs=[
            pl.BlockSpec((q, D_INNER), lambda b, c: (rb + b * nc + c, 0)),
            pl.BlockSpec((1, half, LANES, D_STATE), lambda b, c: (b, 0, 0, 0)),
            pl.BlockSpec((1, SUBLANES, CONV_DIM), lambda b, c: (b, 0, 0)),
        ],
        out_shape=[
            jax.ShapeDtypeStruct((T_ALL, D_INNER), F32),
            jax.ShapeDtypeStruct((nb, half, LANES, D_STATE), F32),
            jax.ShapeDtypeStruct((nb, SUBLANES, CONV_DIM), F32),
        ],
        scratch_shapes=[
            pltpu.VMEM((q + SUBLANES, CONV_DIM), F32),
            pltpu.VMEM((half, LANES, D_STATE), F32),
        ],
        input_output_aliases=aliases,
        compiler_params=_cparams(("parallel", "arbitrary")),
        name="ssd_q%d" % q,
    )(*args)


def _half_rmsnorm(x, gain, lo):
    sq = x * x
    ms_lo = jnp.sum(jnp.where(lo, sq, 0.0), axis=-1, keepdims=True) * (1.0 / SWA_HEAD_DIM)
    ms_hi = jnp.sum(jnp.where(lo, 0.0, sq), axis=-1, keepdims=True) * (1.0 / SWA_HEAD_DIM)
    return x * jnp.where(lo, lax.rsqrt(ms_lo + EPS), lax.rsqrt(ms_hi + EPS)) * gain


def _both_halves(x, lo, take_lo):
    r = pltpu.roll(x, SWA_HEAD_DIM, axis=1)
    return jnp.where(lo, x, r) if take_lo else jnp.where(lo, r, x)


def _pad_rows(x, rows):
    if x.shape[0] == rows:
        return x
    return jnp.concatenate([x, jnp.zeros((rows - x.shape[0], x.shape[1]), x.dtype)], axis=0)


def _swa_kernel(*refs, lq, cached):
    if cached:
        sink_ref, slope_ref, q_ref, kc_ref, vc_ref, kp_ref, vp_ref, gq_ref, gk_ref, _, o_ref, kn_ref = refs
        kp, vp = kp_ref[0], vp_ref[0]
    else:
        sink_ref, slope_ref, q_ref, kc_ref, vc_ref, kp_ref, vp_ref, gq_ref, gk_ref, o_ref, kn_ref = refs
        kp, vp = kp_ref[...], vp_ref[...]
    gq_heads = SWA_HEADS // SWA_KV_HEADS
    rows = gq_heads * lq
    lo_q = lax.broadcasted_iota(jnp.int32, (lq, LANES), 1) < SWA_HEAD_DIM
    lo_w = lax.broadcasted_iota(jnp.int32, (WINDOW, LANES), 1) < SWA_HEAD_DIM
    gq = gq_ref[...]
    gk = gk_ref[...]

    kc = kc_ref[...]
    vc = vc_ref[...]
    kn_slabs, kp_slabs = [], []
    for s in range(2):
        sl = slice(s * LANES, (s + 1) * LANES)
        kn_s = _half_rmsnorm(kc[:, sl], gk, lo_q)
        kn_ref[:, sl] = kn_s
        kn_slabs.append(kn_s)
        kp_slabs.append(kp[:, sl] if cached else _half_rmsnorm(kp[:, sl], gk, lo_w))

    r_idx = lax.broadcasted_iota(jnp.int32, (rows, WINDOW), 0)
    j_idx = lax.broadcasted_iota(jnp.int32, (rows, WINDOW), 1)
    t_idx = r_idx % lq
    mask_prev = j_idx >= t_idx
    if not cached:
        mask_prev = jnp.logical_and(mask_prev, pl.program_id(1) > 0)
    mask_cur = j_idx <= t_idx
    dist_prev = (t_idx + WINDOW - j_idx).astype(F32)
    dist_cur = (t_idx - j_idx).astype(F32)
    head_of_row = lax.broadcasted_iota(jnp.int32, (rows, 1), 0) // lq
    scale = SWA_HEAD_DIM ** -0.5

    for kvh in range(SWA_KV_HEADS):
        slab, take_lo = kvh // 2, (kvh % 2 == 0)
        sl = slice(slab * LANES, (slab + 1) * LANES)
        k_cur = _pad_rows(_both_halves(kn_slabs[slab], lo_q, take_lo), WINDOW).astype(BF16)
        v_cur = _pad_rows(_both_halves(vc[:, sl], lo_q, take_lo), WINDOW).astype(BF16)
        k_prev = _both_halves(kp_slabs[slab], lo_w, take_lo).astype(BF16)
        v_prev = _both_halves(vp[:, sl], lo_w, take_lo).astype(BF16)

        q_parts = []
        for p in range(gq_heads // 2):
            c0 = kvh * gq_heads * SWA_HEAD_DIM + p * LANES
            qn = _half_rmsnorm(q_ref[:, c0:c0 + LANES], gq, lo_q)
            q_parts.append(jnp.where(lo_q, qn, 0.0))
            q_parts.append(jnp.where(lo_q, 0.0, qn))
        qs = jnp.concatenate(q_parts, axis=0).astype(BF16)

        sink = jnp.zeros((rows, 1), F32)
        slope = jnp.zeros((rows, 1), F32)
        for gi in range(gq_heads):
            sink = jnp.where(head_of_row == gi, sink_ref[kvh * gq_heads + gi], sink)
            slope = jnp.where(head_of_row == gi, slope_ref[kvh * gq_heads + gi], slope)

        s_prev = _nt_dot(qs, k_prev) * scale - slope * dist_prev
        s_cur = _nt_dot(qs, k_cur) * scale - slope * dist_cur
        s_prev = jnp.where(mask_prev, s_prev, NEG_INF)
        s_cur = jnp.where(mask_cur, s_cur, NEG_INF)
        m = jnp.maximum(jnp.maximum(jnp.max(s_prev, axis=-1, keepdims=True),
                                    jnp.max(s_cur, axis=-1, keepdims=True)), sink)
        p_prev = jnp.exp(s_prev - m)
        p_cur = jnp.exp(s_cur - m)
        den = (jnp.sum(p_prev, axis=-1, keepdims=True) + jnp.sum(p_cur, axis=-1, keepdims=True)
               + jnp.exp(sink - m))
        o = (jnp.dot(p_prev.astype(BF16), v_prev, preferred_element_type=F32)
             + jnp.dot(p_cur.astype(BF16), v_cur, preferred_element_type=F32)) / den
        for p in range(gq_heads // 2):
            c0 = kvh * gq_heads * SWA_HEAD_DIM + p * LANES
            o_lo = o[(2 * p) * lq:(2 * p + 1) * lq]
            o_hi = o[(2 * p + 1) * lq:(2 * p + 2) * lq]
            o_ref[:, c0:c0 + LANES] = jnp.where(lo_q, o_lo, o_hi)


def _swa(y_in, sinks, slopes, gq2, gk2, win_k, win_v, prev, layer, *, nb, lq, nq, row0):
    rb = row0 // lq
    kw = SWA_KV_HEADS * SWA_HEAD_DIM
    cached = win_k is not None
    in_specs = [
        pl.BlockSpec(memory_space=pltpu.SMEM),
        pl.BlockSpec(memory_space=pltpu.SMEM),
        pl.BlockSpec((lq, D_MODEL), lambda b, i: (rb + b * nq + i, COL_Q // D_MODEL)),
        pl.BlockSpec((lq, kw), lambda b, i: (rb + b * nq + i, COL_K // kw)),
        pl.BlockSpec((lq, kw), lambda b, i: (rb + b * nq + i, COL_V // kw)),
    ]
    args = [sinks, slopes, y_in, y_in, y_in]
    if cached:
        in_specs += [
            pl.BlockSpec((None, 1, WINDOW, kw), lambda b, i: (layer, b, 0, 0)),
            pl.BlockSpec((None, 1, WINDOW, kw), lambda b, i: (layer, b, 0, 0)),
        ]
        args += [win_k, win_v]
    else:
        in_specs += [
            pl.BlockSpec((lq, kw), lambda b, i: (rb + b * nq + jnp.maximum(i - 1, 0), COL_K // kw)),
            pl.BlockSpec((lq, kw), lambda b, i: (rb + b * nq + jnp.maximum(i - 1, 0), COL_V // kw)),
        ]
        args += [y_in, y_in]
    in_specs += [pl.BlockSpec((1, LANES), lambda b, i: (0, 0)), pl.BlockSpec((1, LANES), lambda b, i: (0, 0))]
    args += [gq2, gk2]
    aliases = {}
    if cached:
        in_specs.append(pl.BlockSpec(memory_space=pl.ANY))
        args.append(prev)
        aliases = {len(args) - 1: 0}
    return pl.pallas_call(
        functools.partial(_swa_kernel, lq=lq, cached=cached),
        grid=(nb, nq),
        in_specs=in_specs,
        out_specs=[
            pl.BlockSpec((lq, D_MODEL), lambda b, i: (rb + b * nq + i, 0)),
            pl.BlockSpec((lq, kw), lambda b, i: (b * nq + i, 0)),
        ],
        out_shape=[
            jax.ShapeDtypeStruct((T_ALL, D_MODEL), F32),
            jax.ShapeDtypeStruct((nb * nq * lq, kw), F32),
        ],
        input_output_aliases=aliases,
        compiler_params=_cparams(("parallel", "arbitrary")),
        name="swa_lq%d" % lq,
    )(*args)


def _mem_attn_kernel(*refs, cached):
    if cached:
        q_ref, k_ref, v_ref, _, o_ref = refs
    else:
        q_ref, k_ref, v_ref, o_ref = refs
    scale = MEM_HEAD_DIM ** -0.5
    for h in range(MEM_HEADS):
        sl = slice(h * MEM_HEAD_DIM, (h + 1) * MEM_HEAD_DIM)
        qh = q_ref[:, sl].astype(BF16)
        if cached:
            kh = k_ref[0, :, sl].astype(BF16)
            vh = v_ref[0, :, sl].astype(BF16)
        else:
            kh = k_ref[:, sl].astype(BF16)
            vh = v_ref[:, sl].astype(BF16)
        s = _nt_dot(qh, kh) * scale
        m = jnp.max(s, axis=-1, keepdims=True)
        p = jnp.exp(s - m)
        den = jnp.sum(p, axis=-1, keepdims=True)
        o_ref[:, sl] = jnp.dot(p.astype(BF16), vh, preferred_element_type=F32) / den


def _mem_attn(y_in, mem_k, mem_v, prev, layer, *, nb, tq, nq, row0):
    rb = row0 // tq
    cached = prev is not None
    in_specs = [pl.BlockSpec((tq, D_MODEL), lambda b, i: (rb + b * nq + i, COL_QM // D_MODEL))]
    args = [y_in, mem_k, mem_v]
    aliases = {}
    if cached:
        in_specs += [
            pl.BlockSpec((None, 1, N_MEM, D_MODEL), lambda b, i: (layer, b, 0, 0)),
            pl.BlockSpec((None, 1, N_MEM, D_MODEL), lambda b, i: (layer, b, 0, 0)),
            pl.BlockSpec(memory_space=pl.ANY),
        ]
        args.append(prev)
        aliases = {len(args) - 1: 0}
    else:
        in_specs += [
            pl.BlockSpec((N_MEM, D_MODEL), lambda b, i: (b, 0)),
            pl.BlockSpec((N_MEM, D_MODEL), lambda b, i: (b, 1)),
        ]
    return pl.pallas_call(
        functools.partial(_mem_attn_kernel, cached=cached),
        grid=(nb, nq),
        in_specs=in_specs,
        out_specs=pl.BlockSpec((tq, D_MODEL), lambda b, i: (rb + b * nq + i, 0)),
        out_shape=jax.ShapeDtypeStruct((T_ALL, D_MODEL), F32),
        input_output_aliases=aliases,
        compiler_params=_cparams(("parallel", "arbitrary")),
        name="mem_attn_tq%d" % tq,
    )(*args)


def _merge_kernel(a0_ref, a1_ref, a2_ref, w0_ref, w1_ref, w2_ref, g0_ref, g1_ref, g2_ref,
                  b0_ref, b1_ref, b2_ref, o_ref, ab_ref):
    @pl.when(pl.program_id(1) == 0)
    def _():
        for k, a_ref in enumerate((a0_ref, a1_ref, a2_ref)):
            ab_ref[k] = a_ref[...].astype(BF16)

    acc = None
    for k, (w_ref, g_ref, b_ref) in enumerate(((w0_ref, g0_ref, b0_ref), (w1_ref, g1_ref, b1_ref),
                                               (w2_ref, g2_ref, b2_ref))):
        br = jnp.dot(ab_ref[k], w_ref[...], preferred_element_type=F32)
        t = jax.nn.sigmoid(g_ref[...] + b_ref[...]) * br
        acc = t if acc is None else acc + t
    o_ref[...] = acc.astype(BF16)


def _merge(a0, a1, a2, w0, w1, w2, y_in, b_gate, layer, *, tm, tn):
    gb = COL_GATE // tn
    nbk = D_MODEL // tn
    a_spec = pl.BlockSpec((tm, D_MODEL), lambda i, j: (i, 0), pipeline_mode=pl.Buffered(1))
    w_spec = pl.BlockSpec((None, D_MODEL, tn), lambda i, j: (layer, 0, j))
    return pl.pallas_call(
        _merge_kernel,
        grid=(T_ALL // tm, nbk),
        in_specs=[a_spec, a_spec, a_spec, w_spec, w_spec, w_spec,
                  pl.BlockSpec((tm, tn), lambda i, j: (i, gb + j)),
                  pl.BlockSpec((tm, tn), lambda i, j: (i, gb + nbk + j)),
                  pl.BlockSpec((tm, tn), lambda i, j: (i, gb + 2 * nbk + j)),
                  pl.BlockSpec((1, tn), lambda i, j: (0, j)),
                  pl.BlockSpec((1, tn), lambda i, j: (0, nbk + j)),
                  pl.BlockSpec((1, tn), lambda i, j: (0, 2 * nbk + j))],
        out_specs=pl.BlockSpec((tm, tn), lambda i, j: (i, j)),
        out_shape=jax.ShapeDtypeStruct((T_ALL, D_MODEL), BF16),
        scratch_shapes=[pltpu.VMEM((N_BRANCH, tm, D_MODEL), BF16)],
        compiler_params=_cparams(("parallel", "arbitrary")),
        name="gated_merge",
    )(a0, a1, a2, w0, w1, w2, y_in, y_in, y_in, b_gate, b_gate, b_gate)


def _matmul_res_kernel(a_ref, w_ref, r_ref, o_ref):
    o_ref[...] = r_ref[...] + jnp.dot(a_ref[...], w_ref[...], preferred_element_type=F32)


def _matmul_res(a, w, r, layer, *, tn):
    m, k = a.shape
    n = w.shape[2]
    return pl.pallas_call(
        _matmul_res_kernel,
        grid=(m // TM, n // tn),
        in_specs=[pl.BlockSpec((TM, k), lambda i, j: (i, 0)),
                  pl.BlockSpec((None, k, tn), lambda i, j: (layer, 0, j)),
                  pl.BlockSpec((TM, tn), lambda i, j: (i, j))],
        out_specs=pl.BlockSpec((TM, tn), lambda i, j: (i, j)),
        out_shape=jax.ShapeDtypeStruct((m, n), F32),
        compiler_params=_cparams(("parallel", "arbitrary")),
        name="matmul_residual",
    )(a, w, r)


def _top_values(s, k):
    vals = []
    cur = s
    for r in range(k):
        mx = jnp.max(cur, axis=0, keepdims=True)
        vals.append(mx)
        if r + 1 < k:
            cur = jnp.where(cur == mx, NEG_INF, cur)
    return vals


def _router_kernel(x_ref, g_ref, wqt_ref, keys_ref, h2_ref, e0_ref, th_ref, e1_ref):
    h2 = _rms(x_ref[...], g_ref[...]).astype(BF16)
    h2_ref[...] = h2
    qt = _nt_dot(wqt_ref[...], h2)
    for h in range(PEER_HEADS):
        s0 = jnp.dot(keys_ref[2 * h], qt[(2 * h) * PEER_HALF:(2 * h + 1) * PEER_HALF, :].astype(BF16),
                     preferred_element_type=F32)
        s1 = jnp.dot(keys_ref[2 * h + 1], qt[(2 * h + 1) * PEER_HALF:(2 * h + 2) * PEER_HALF, :].astype(BF16),
                     preferred_element_type=F32)
        top0 = _top_values(s0, PEER_TOPK)
        top1 = _top_values(s1, PEER_TOPK)
        a1 = jnp.concatenate(top1, axis=0)
        half_k = PEER_TOPK // 2
        cand = jnp.concatenate(
            [top0[0] + a1]
            + [top0[r] + a1[:half_k] for r in range(1, half_k)]
            + [jnp.concatenate(top0[half_k:], axis=0) + top1[0]], axis=0)
        best = _top_values(cand, PEER_TOPK)
        m = best[0]
        tau = best[PEER_TOPK - 1]
        z = jnp.zeros_like(m)
        for r in range(PEER_TOPK):
            z = z + jnp.exp(best[r] - m)
        e1_sorted = jnp.exp(a1 - top1[0])
        th = jnp.full(s0.shape, NEVER, F32)
        for r in range(PEER_TOPK):
            th_r = jnp.min(jnp.where((top0[r] + a1) >= tau, e1_sorted, NEVER), axis=0, keepdims=True)
            th = jnp.where(s0 == top0[r], th_r, th)
        e0_ref[h] = 0.5 * jnp.exp(s0 - top0[0]) / z
        th_ref[h] = th
        e1_ref[h] = jnp.exp(s1 - top1[0])


def _router(x, g, wqt, keys, layer, *, tmr):
    t = x.shape[0]
    tok = pl.BlockSpec((PEER_HEADS, N_KEYS, tmr), lambda i: (0, 0, i))
    per_key = jax.ShapeDtypeStruct((PEER_HEADS, N_KEYS, t), F32)
    return pl.pallas_call(
        _router_kernel,
        grid=(t // tmr,),
        in_specs=[pl.BlockSpec((tmr, D_MODEL), lambda i: (i, 0)),
                  pl.BlockSpec((1, D_MODEL), lambda i: (0, 0)),
                  pl.BlockSpec((None, PEER_HEADS * PEER_QUERY_DIM, D_MODEL), lambda i: (layer, 0, 0)),
                  pl.BlockSpec((None, 2 * PEER_HEADS, N_KEYS, PEER_HALF), lambda i: (layer, 0, 0, 0))],
        out_specs=[pl.BlockSpec((tmr, D_MODEL), lambda i: (i, 0)), tok, tok, tok],
        out_shape=[jax.ShapeDtypeStruct((t, D_MODEL), BF16), per_key, per_key, per_key],
        compiler_params=_cparams(("parallel",)),
        name="peer_router",
    )(x, g, wqt, keys)


def _peer_dense_kernel(h2_ref, u_ref, vt_ref, e0_ref, th_ref, e1_ref, x_ref, o_ref,
                       acc_ref, at_ref, wa_ref, wb_ref, h2t_ref, *, te, ne, tm):
    e = pl.program_id(1)
    per_step = te // N_KEYS
    assert 2 * per_step == SUBLANES

    @pl.when(e == 0)
    def _():
        acc_ref[...] = jnp.zeros_like(acc_ref)
        wb_ref[...] = jnp.zeros_like(wb_ref)
        h2t_ref[...] = h2_ref[...].T

    chunk = 2 * LANES
    n_chunks = tm // chunk
    vrows = D_MODEL // (2 * per_step)

    def value_piece(w_old_ref, c, m):
        rows = slice(m * vrows, (m + 1) * vrows)
        cols = slice(c * chunk, (c + 1) * chunk)
        acc_ref[rows, cols] += jnp.dot(vt_ref[rows, :], w_old_ref[:, cols], preferred_element_type=F32)

    def step(w_new_ref, w_old_ref, parity):
        def score_piece(c, ii):
            rows = slice(ii * N_KEYS, (ii + 1) * N_KEYS)
            cols = slice(c * chunk, (c + 1) * chunk)
            at_ref[rows, cols] = jnp.dot(u_ref[rows, :], h2t_ref[:, cols], preferred_element_type=F32)

        def gate_cell(ii, tg):
            rows = slice(ii * N_KEYS, (ii + 1) * N_KEYS)
            cs = slice(tg * LANES, (tg + 1) * LANES)
            r = parity * per_step + ii
            gate = None
            for h in range(PEER_HEADS):
                e1 = e1_ref[h, :, cs]
                t = jnp.where(e1 >= th_ref[h, r:r + 1, cs], e1 * e0_ref[h, r:r + 1, cs], 0.0)
                gate = t if gate is None else gate + t
            a = at_ref[rows, cs]
            gelu2 = a * (1.0 + lax.erf(a * (2.0 ** -0.5)))
            w_new_ref[rows, cs] = (gate * gelu2).astype(BF16)

        rounds = [(c, ii) for c in range(n_chunks) for ii in range(per_step)]
        score_piece(*rounds[0])
        for k, (c, ii) in enumerate(rounds):
            if k + 1 < len(rounds):
                score_piece(*rounds[k + 1])
            gate_cell(ii, 2 * c)
            value_piece(w_old_ref, c, 2 * ii)
            gate_cell(ii, 2 * c + 1)
            value_piece(w_old_ref, c, 2 * ii + 1)

    @pl.when(jnp.logical_and(e % 2 == 0, e < ne))
    def _():
        step(wa_ref, wb_ref, 0)

    @pl.when(jnp.logical_and(e % 2 == 1, e < ne))
    def _():
        step(wb_ref, wa_ref, 1)

    @pl.when(e == ne)
    def _():
        w_last_ref = wb_ref if ne % 2 == 0 else wa_ref
        for c in range(n_chunks):
            for m in range(D_MODEL // vrows):
                value_piece(w_last_ref, c, m)
        o_ref[...] = x_ref[...] + acc_ref[...].T


def _peer_dense(h2, u, vt, e0, th, e1, x, layer, *, te, tm):
    t = x.shape[0]
    ne = N_EXPERTS // te
    once = pl.Buffered(1)
    by_first = pl.BlockSpec((PEER_HEADS, SUBLANES, tm), lambda i, e: (0, jnp.minimum(e, ne - 1) // 2, i))
    by_second = pl.BlockSpec((PEER_HEADS, N_KEYS, tm), lambda i, e: (0, 0, i), pipeline_mode=once)
    return pl.pallas_call(
        functools.partial(_peer_dense_kernel, te=te, ne=ne, tm=tm),
        grid=(t // tm, ne + 1),
        in_specs=[pl.BlockSpec((tm, D_MODEL), lambda i, e: (i, 0), pipeline_mode=once),
                  pl.BlockSpec((None, te, D_MODEL), lambda i, e: (layer, jnp.minimum(e, ne - 1), 0)),
                  pl.BlockSpec((None, D_MODEL, te), lambda i, e: (layer, 0, jnp.maximum(e - 1, 0))),
                  by_first, by_first, by_second,
                  pl.BlockSpec((tm, D_MODEL), lambda i, e: (i, 0), pipeline_mode=once)],
        out_specs=pl.BlockSpec((tm, D_MODEL), lambda i, e: (i, 0)),
        out_shape=jax.ShapeDtypeStruct((t, D_MODEL), F32),
        scratch_shapes=[pltpu.VMEM((D_MODEL, tm), F32), pltpu.VMEM((te, tm), F32),
                        pltpu.VMEM((te, tm), BF16), pltpu.VMEM((te, tm), BF16),
                        pltpu.VMEM((D_MODEL, tm), BF16)],
        compiler_params=_cparams(("parallel", "arbitrary")),
        name="peer_dense",
    )(h2, u, vt, e0, th, e1, x)


def _w_in_column_map():
    src_col = {}
    z0, xbc0 = 0, D_INNER
    dt0 = xbc0 + CONV_DIM
    q0 = dt0 + SSD_HEADS
    k0 = q0 + D_MODEL
    v0 = k0 + 256
    qm0 = v0 + 256
    gate0 = qm0 + D_MODEL
    for dst, src, width in ((COL_XBC, xbc0, CONV_DIM), (COL_K, k0, 256), (COL_V, v0, 256),
                            (COL_DT, dt0, LANES), (COL_DT + LANES, dt0, LANES), (COL_DT + 2 * LANES, dt0, LANES),
                            (COL_DT + 3 * LANES, dt0, LANES), (COL_Z, z0, D_INNER), (COL_Q, q0, D_MODEL),
                            (COL_QM, qm0, D_MODEL), (COL_GATE, gate0, N_BRANCH * D_MODEL)):
        for u in range(width // LANES):
            src_col[dst // LANES + u] = src + u * LANES
    units = [src_col[d] // LANES for d in range(IN_PAD // LANES)]
    shifts = [src_col[d] % LANES for d in range(IN_PAD // LANES)]
    return units, shifts


def _prep_w_in_kernel(unit_ref, shift_ref, a_ref, b_ref, o_ref):
    d = pl.program_id(1)
    shift = shift_ref[d]

    @pl.when(shift == 0)
    def _():
        o_ref[...] = a_ref[...].astype(BF16)

    @pl.when(shift != 0)
    def _():
        lane = lax.broadcasted_iota(jnp.int32, a_ref.shape, 1)
        keep = LANES - SSD_HEADS
        moved = jnp.where(lane < keep, pltpu.roll(a_ref[...], keep, axis=1), pltpu.roll(b_ref[...], keep, axis=1))
        o_ref[...] = moved.astype(BF16)


def _prep_w_in(w):
    units, shifts = _w_in_column_map()
    assert all(s in (0, SSD_HEADS) for s in shifts)
    grid_spec = pltpu.PrefetchScalarGridSpec(
        num_scalar_prefetch=2,
        grid=(DEPTH, IN_PAD // LANES),
        in_specs=[pl.BlockSpec((None, D_MODEL, LANES), lambda l, d, un, sh: (l, 0, un[d])),
                  pl.BlockSpec((None, D_MODEL, LANES), lambda l, d, un, sh: (l, 0, un[d] + 1))],
        out_specs=pl.BlockSpec((None, D_MODEL, LANES), lambda l, d, un, sh: (l, 0, d)),
    )
    return pl.pallas_call(
        _prep_w_in_kernel,
        grid_spec=grid_spec,
        out_shape=jax.ShapeDtypeStruct((DEPTH, D_MODEL, IN_PAD), BF16),
        compiler_params=_cparams(("parallel", "arbitrary")),
        name="prep_w_in",
    )(jnp.asarray(units, jnp.int32), jnp.asarray(shifts, jnp.int32), w, w)


def _pad_lanes(v, n=LANES):
    return jnp.pad(v, (0, n - v.shape[0])).reshape(1, n)


def kernel(x_prompt, x_sample, cache_win_k, cache_win_v, state_ssm, state_conv, cache_mem_k, cache_mem_v,
           mem_prompt, w_in, b_gate, conv_w, conv_b, dt_bias, a_log, d_skip, g_ssd_norm, g_q, g_k,
           attn_sinks, g_mem, w_mem_kv, g_qm, g_km, w_o_ssm, w_o_swa, w_o_mem, w_out, g_mix, g_ffn,
           w_peer_q, peer_sub_keys, peer_u, peer_v):
    x = jnp.concatenate([x_prompt.reshape(T_PROMPT, D_MODEL), x_sample.reshape(T_SAMPLE, D_MODEL)], axis=0)
    slopes = jnp.exp2(-8.0 * jnp.arange(1, SWA_HEADS + 1, dtype=F32) / SWA_HEADS)
    mem_x = mem_prompt.reshape(BATCH * N_MEM, D_MODEL)
    kw = SWA_KV_HEADS * SWA_HEAD_DIM
    half = SSD_HEADS // 2
    row = lambda v: v.reshape(1, -1)

    w_in_b = _prep_w_in(w_in)
    w_mem_kv_b = w_mem_kv.astype(BF16)
    w_o_ssm_b, w_o_swa_b, w_o_mem_b = w_o_ssm.astype(BF16), w_o_swa.astype(BF16), w_o_mem.astype(BF16)
    w_out_b = w_out.astype(BF16)
    wqt_b = jnp.swapaxes(w_peer_q, 1, 2).astype(BF16)
    keys_b = peer_sub_keys.reshape(DEPTH, 2 * PEER_HEADS, N_KEYS, PEER_HALF).astype(BF16)
    u_b = peer_u.astype(BF16)
    vt_b = jnp.swapaxes(peer_v, 1, 2).astype(BF16)
    conv0 = jnp.pad(state_conv, ((0, 0), (0, 0), (SUBLANES - (CONV_W - 1), 0), (0, 0)))
    ssm0 = state_ssm.reshape(DEPTH, DEC_BATCH, half, LANES, D_STATE)
    wk_c = cache_win_k.reshape(DEPTH, DEC_BATCH, WINDOW, kw)
    wv_c = cache_win_v.reshape(DEPTH, DEC_BATCH, WINDOW, kw)
    mk_c = cache_mem_k.reshape(DEPTH, DEC_BATCH, N_MEM, D_MODEL)
    mv_c = cache_mem_v.reshape(DEPTH, DEC_BATCH, N_MEM, D_MODEL)

    outs = {k: [] for k in ("p_wk", "p_wv", "p_ssm", "p_conv", "p_mk", "p_mv", "s_wk", "s_wv", "s_ssm", "s_conv")}
    for l in range(DEPTH):
        mem_kv = _norm_matmul(mem_x, row(g_mem[l]), w_mem_kv_b, row(g_km[l]), l,
                              tm=512, tn=MEM_HEAD_DIM, norm_lo=0, norm_hi=MEM_HEADS)
        outs["p_mk"].append(mem_kv[:, :D_MODEL].reshape(BATCH, N_MEM, MEM_HEADS, MEM_HEAD_DIM))
        outs["p_mv"].append(mem_kv[:, D_MODEL:].reshape(BATCH, N_MEM, MEM_HEADS, MEM_HEAD_DIM))

        tn_in = 1024
        y = _norm_matmul(x, row(g_mix[l]), w_in_b, row(g_qm[l]), l, tm=TM, tn=tn_in,
                         norm_lo=COL_QM // tn_in, norm_hi=COL_GATE // tn_in)

        cw, cb = conv_w[l], row(conv_b[l])
        dtb, alog = _pad_lanes(dt_bias[l]), _pad_lanes(a_log[l])
        dskip = row(jnp.repeat(d_skip[l], SSD_HEAD_DIM))
        gn = row(g_ssd_norm[l])
        y_ssm, p_ssm, p_tail = _ssd(y, cw, cb, dtb, alog, dskip, gn, None, None, None, l,
                                    nb=BATCH, q=SSD_CHUNK, nc=SEQ // SSD_CHUNK, row0=0)
        y_ssm, s_ssm, s_tail = _ssd(y, cw, cb, dtb, alog, dskip, gn, conv0, ssm0, y_ssm, l,
                                    nb=DEC_BATCH, q=DEC_SEQ, nc=1, row0=T_PROMPT)
        outs["p_ssm"].append(p_ssm.reshape(BATCH, SSD_HEADS, SSD_HEAD_DIM, D_STATE))
        outs["s_ssm"].append(s_ssm.reshape(DEC_BATCH, SSD_HEADS, SSD_HEAD_DIM, D_STATE))
        outs["p_conv"].append(p_tail[:, SUBLANES - (CONV_W - 1):])
        outs["s_conv"].append(s_tail[:, SUBLANES - (CONV_W - 1):])

        gq2 = row(jnp.concatenate([g_q[l], g_q[l]]))
        gk2 = row(jnp.concatenate([g_k[l], g_k[l]]))
        o_swa, kn_p = _swa(y, attn_sinks[l], slopes, gq2, gk2, None, None, None, l,
                           nb=BATCH, lq=WINDOW, nq=SEQ // WINDOW, row0=0)
        o_swa, kn_s = _swa(y, attn_sinks[l], slopes, gq2, gk2, wk_c, wv_c, o_swa, l,
                           nb=DEC_BATCH, lq=DEC_SEQ, nq=1, row0=T_PROMPT)
        v_p = y[:T_PROMPT, COL_V:COL_V + kw].reshape(BATCH, SEQ, kw)
        v_s = y[T_PROMPT:, COL_V:COL_V + kw].reshape(DEC_BATCH, DEC_SEQ, kw)
        kv_shape = (SWA_KV_HEADS, SWA_HEAD_DIM)
        outs["p_wk"].append(kn_p.reshape(BATCH, SEQ, kw)[:, SEQ - WINDOW:].reshape(BATCH, WINDOW, *kv_shape))
        outs["p_wv"].append(v_p[:, SEQ - WINDOW:].reshape(BATCH, WINDOW, *kv_shape))
        outs["s_wk"].append(jnp.concatenate([wk_c[l][:, DEC_SEQ:], kn_s.reshape(DEC_BATCH, DEC_SEQ, kw)], axis=1)
                            .reshape(DEC_BATCH, WINDOW, *kv_shape))
        outs["s_wv"].append(jnp.concatenate([wv_c[l][:, DEC_SEQ:], v_s], axis=1)
                            .reshape(DEC_BATCH, WINDOW, *kv_shape))

        o_mem = _mem_attn(y, mem_kv, mem_kv, None, l, nb=BATCH, tq=512, nq=SEQ // 512, row0=0)
        o_mem = _mem_attn(y, mk_c, mv_c, o_mem, l, nb=DEC_BATCH, tq=DEC_SEQ, nq=1, row0=T_PROMPT)

        merged = _merge(y_ssm, o_swa, o_mem, w_o_ssm_b, w_o_swa_b, w_o_mem_b, y, row(b_gate[l]), l,
                        tm=TM, tn=512)
        x = _matmul_res(merged, w_out_b, x, l, tn=512)

        h2, e0, th, e1 = _router(x, row(g_ffn[l]), wqt_b, keys_b, l, tmr=256)
        x = _peer_dense(h2, u_b, vt_b, e0, th, e1, x, l, te=512, tm=TM)

    yp = x[:T_PROMPT].reshape(BATCH, SEQ, D_MODEL)
    ys = x[T_PROMPT:].reshape(DEC_BATCH, DEC_SEQ, D_MODEL)
    st = lambda k: jnp.stack(outs[k])
    return (yp, ys, st("p_wk"), st("p_wv"), st("p_ssm"), st("p_conv"), st("p_mk"), st("p_mv"),
            st("s_wk"), st("s_wv"), st("s_ssm"), st("s_conv"))
```

```python
import functools

import jax
import jax.numpy as jnp
from jax import lax
from jax.experimental import pallas as pl
from jax.experimental.pallas import tpu as pltpu

F32 = jnp.float32
BF16 = jnp.bfloat16

D_MODEL = 2048
BATCH = 4
SEQ = 2048
DEPTH = 2
DEC_BATCH = 32
DEC_SEQ = 8
D_INNER = D_MODEL
SSD_HEAD_DIM = 64
SSD_HEADS = D_INNER // SSD_HEAD_DIM
SSD_GROUPS = 4
D_STATE = 128
CONV_W = 4
CONV_DIM = D_INNER + 2 * SSD_GROUPS * D_STATE
SSD_CHUNK = 128
SWA_HEAD_DIM = 64
SWA_HEADS = D_MODEL // SWA_HEAD_DIM
SWA_KV_HEADS = SWA_HEADS // 8
WINDOW = 128
N_MEM = 256
MEM_HEADS = 4
MEM_HEAD_DIM = D_MODEL // MEM_HEADS
PEER_HEADS = 8
N_KEYS = 128
N_EXPERTS = N_KEYS * N_KEYS
PEER_QUERY_DIM = 256
PEER_HALF = PEER_QUERY_DIM // 2
PEER_TOPK = 16
N_BRANCH = 3
EPS = 1e-6

LANES = 128
SUBLANES = 8
VMEM_LIMIT = 56 * 1024 * 1024

T_PROMPT = BATCH * SEQ
T_SAMPLE = DEC_BATCH * DEC_SEQ
T_ALL = T_PROMPT + T_SAMPLE
TM = 768

COL_XBC = 0
COL_K = CONV_DIM
COL_V = COL_K + 256
COL_DT = COL_V + 256
COL_Z = 4096
COL_Q = COL_Z + D_INNER
COL_QM = COL_Q + D_MODEL
COL_GATE = COL_QM + D_MODEL
IN_PAD = COL_GATE + N_BRANCH * D_MODEL

NEG_INF = float("-inf")
NEVER = 2.0


def _cparams(sem, flags=None):
    return pltpu.CompilerParams(dimension_semantics=sem, vmem_limit_bytes=VMEM_LIMIT, flags=flags)


def _nt_dot(a, b):
    return lax.dot_general(a, b, (((1,), (1,)), ((), ())), preferred_element_type=F32)


def _tn_dot(a, b):
    return lax.dot_general(a, b, (((0,), (0,)), ((), ())), preferred_element_type=F32)


def _rms(y, gain):
    ms = jnp.mean(y * y, axis=-1, keepdims=True)
    return y * lax.rsqrt(ms + EPS) * gain


def _norm_matmul_kernel(x_ref, g_ref, w_ref, gn_ref, o_ref, h_ref, *, norm_lo, norm_hi):
    n = pl.program_id(1)

    @pl.when(n == 0)
    def _():
        h_ref[...] = _rms(x_ref[...], g_ref[...]).astype(BF16)

    y = jnp.dot(h_ref[...], w_ref[...], preferred_element_type=F32)
    is_norm = jnp.logical_and(n >= norm_lo, n < norm_hi)

    @pl.when(is_norm)
    def _():
        for c in range(y.shape[1] // MEM_HEAD_DIM):
            sl = slice(c * MEM_HEAD_DIM, (c + 1) * MEM_HEAD_DIM)
            o_ref[:, sl] = _rms(y[:, sl], gn_ref[...])

    @pl.when(jnp.logical_not(is_norm))
    def _():
        o_ref[...] = y


def _norm_matmul(x, g, w, gn, layer, *, tm, tn, norm_lo, norm_hi):
    m, k = x.shape
    n = w.shape[2]
    return pl.pallas_call(
        functools.partial(_norm_matmul_kernel, norm_lo=norm_lo, norm_hi=norm_hi),
        grid=(m // tm, n // tn),
        in_specs=[
            pl.BlockSpec((tm, k), lambda i, j: (i, 0)),
            pl.BlockSpec((1, k), lambda i, j: (0, 0)),
            pl.BlockSpec((None, k, tn), lambda i, j: (layer, 0, j)),
            pl.BlockSpec((1, MEM_HEAD_DIM), lambda i, j: (0, 0)),
        ],
        out_specs=pl.BlockSpec((tm, tn), lambda i, j: (i, j)),
        out_shape=jax.ShapeDtypeStruct((m, n), F32),
        scratch_shapes=[pltpu.VMEM((tm, k), BF16)],
        compiler_params=_cparams(("parallel", "arbitrary")),
        name="norm_matmul",
    )(x, g, w, gn)


IN_UNIT = 256
IN_TILE = 1024


def _w_in_row_map():
    z0, xbc0 = 0, D_INNER
    dt0 = xbc0 + CONV_DIM
    q0 = dt0 + SSD_HEADS
    k0 = q0 + D_MODEL
    v0 = k0 + 256
    qm0 = v0 + 256
    gate0 = qm0 + D_MODEL
    rows = {}
    for dst, src, width in ((COL_XBC, xbc0, CONV_DIM), (COL_K, k0, 256), (COL_V, v0, 256),
                            (COL_DT, dt0, IN_UNIT), (COL_DT + IN_UNIT, dt0, IN_UNIT), (COL_Z, z0, D_INNER),
                            (COL_Q, q0, D_MODEL), (COL_QM, qm0, D_MODEL), (COL_GATE, gate0, N_BRANCH * D_MODEL)):
        for u in range(width // IN_UNIT):
            rows[dst // IN_UNIT + u] = src + u * IN_UNIT
    return [rows[d] for d in range(IN_PAD // IN_UNIT)]


def _rms_cast_kernel(x_ref, g_ref, o_ref):
    o_ref[...] = _rms(x_ref[...], g_ref[...]).astype(BF16)


def _rms_cast(x, g):
    m, k = x.shape
    return pl.pallas_call(
        _rms_cast_kernel,
        grid=(m // TM,),
        in_specs=[pl.BlockSpec((TM, k), lambda i: (i, 0)), pl.BlockSpec((1, k), lambda i: (0, 0))],
        out_specs=pl.BlockSpec((TM, k), lambda i: (i, 0)),
        out_shape=jax.ShapeDtypeStruct((m, k), BF16),
        compiler_params=_cparams(("parallel",)),
        name="rms_cast",
    )(x, g)


def _in_proj_kernel(rows_ref, h_ref, w0_ref, w1_ref, w2_ref, w3_ref, gn_ref, o_ref, wb_ref, *, norm_lo, norm_hi):
    del rows_ref
    n = pl.program_id(0)

    @pl.when(pl.program_id(1) == 0)
    def _():
        for u, w_ref in enumerate((w0_ref, w1_ref, w2_ref, w3_ref)):
            wb_ref[u * IN_UNIT:(u + 1) * IN_UNIT, :] = w_ref[...].astype(BF16)

    y = _nt_dot(h_ref[...], wb_ref[...])
    is_norm = jnp.logical_and(n >= norm_lo, n < norm_hi)

    @pl.when(is_norm)
    def _():
        for c in range(IN_TILE // MEM_HEAD_DIM):
            sl = slice(c * MEM_HEAD_DIM, (c + 1) * MEM_HEAD_DIM)
            o_ref[:, sl] = _rms(y[:, sl], gn_ref[...])

    @pl.when(jnp.logical_not(is_norm))
    def _():
        o_ref[...] = y


def _in_proj(h, w_t, gn, layer):
    t = h.shape[0]
    per_tile = IN_TILE // IN_UNIT

    def w_spec(u):
        return pl.BlockSpec((None, pl.Element(IN_UNIT), pl.Element(D_MODEL)),
                            lambda n, m, rows: (layer, pl.multiple_of(rows[per_tile * n + u], SSD_HEADS), 0))

    grid_spec = pltpu.PrefetchScalarGridSpec(
        num_scalar_prefetch=1,
        grid=(IN_PAD // IN_TILE, t // TM),
        in_specs=[pl.BlockSpec((TM, D_MODEL), lambda n, m, rows: (m, 0))]
        + [w_spec(u) for u in range(per_tile)]
        + [pl.BlockSpec((1, MEM_HEAD_DIM), lambda n, m, rows: (0, 0))],
        out_specs=pl.BlockSpec((TM, IN_TILE), lambda n, m, rows: (m, n)),
        scratch_shapes=[pltpu.VMEM((IN_TILE, D_MODEL), BF16)],
    )
    return pl.pallas_call(
        functools.partial(_in_proj_kernel, norm_lo=COL_QM // IN_TILE, norm_hi=COL_GATE // IN_TILE),
        grid_spec=grid_spec,
        out_shape=jax.ShapeDtypeStruct((t, IN_PAD), F32),
        compiler_params=_cparams(("parallel", "arbitrary")),
        name="in_proj",
    )(jnp.asarray(_w_in_row_map(), jnp.int32), h, w_t, w_t, w_t, w_t, gn)


def _ssd_kernel(*refs, q, nc, has_init):
    if has_init:
        (xbc_ref, z_ref, dt_ref, cw_ref, cb_ref, dtb_ref, alog_ref, dskip_ref, gn_ref, conv0_ref, ssm0_ref,
         _, y_ref, ssm_out_ref, tail_ref, ext_ref, s_ref) = refs
    else:
        (xbc_ref, z_ref, dt_ref, cw_ref, cb_ref, dtb_ref, alog_ref, dskip_ref, gn_ref,
         y_ref, ssm_out_ref, tail_ref, ext_ref, s_ref) = refs
    c = pl.program_id(1)

    @pl.when(c == 0)
    def _():
        if has_init:
            ext_ref[0:SUBLANES, :] = conv0_ref[0]
            s_ref[...] = ssm0_ref[0]
        else:
            ext_ref[0:SUBLANES, :] = jnp.zeros((SUBLANES, CONV_DIM), F32)
            s_ref[...] = jnp.zeros_like(s_ref)

    ext_ref[SUBLANES:SUBLANES + q, :] = xbc_ref[...]
    w = cw_ref[...]
    conv = (cb_ref[...]
            + w[3:4] * ext_ref[SUBLANES:SUBLANES + q, :]
            + w[2:3] * ext_ref[SUBLANES - 1:SUBLANES - 1 + q, :]
            + w[1:2] * ext_ref[SUBLANES - 2:SUBLANES - 2 + q, :]
            + w[0:1] * ext_ref[SUBLANES - 3:SUBLANES - 3 + q, :])
    tail = ext_ref[q:q + SUBLANES, :]
    ext_ref[0:SUBLANES, :] = tail
    xc = conv * jax.nn.sigmoid(conv)

    gn_w = SSD_GROUPS * D_STATE
    bm = xc[:, D_INNER:D_INNER + gn_w].astype(BF16)
    cm = xc[:, D_INNER + gn_w:].astype(BF16)

    dtr = dt_ref[...] + dtb_ref[...]
    dt = jnp.maximum(dtr, 0.0) + jnp.log1p(jnp.exp(-jnp.abs(dtr)))
    a_neg = -jnp.exp(alog_ref[...])
    da = dt * a_neg
    row = lax.broadcasted_iota(jnp.int32, (q, q), 0)
    col = lax.broadcasted_iota(jnp.int32, (q, q), 1)
    causal = row >= col
    tri = causal.astype(F32)
    cum = jnp.dot(tri, da, preferred_element_type=F32, precision=lax.Precision.HIGHEST)
    if q < LANES:
        cum_sq = jnp.concatenate([cum, jnp.zeros((LANES - q, LANES), F32)], axis=0)
    else:
        cum_sq = cum
    cum_t = cum_sq.T[:, :q]
    ecum = jnp.exp(cum)
    cl = cum[q - 1:q, :]
    wend = jnp.exp(cl - cum)
    ecl = jnp.exp(cl)

    lo = lax.broadcasted_iota(jnp.int32, (q, LANES), 1) < SSD_HEAD_DIM
    row_lo = lax.broadcasted_iota(jnp.int32, (LANES, LANES), 0) < SSD_HEAD_DIM
    pairs_per_group = SSD_HEADS // SSD_GROUPS // 2

    for g in range(SSD_GROUPS):
        bg = bm[:, g * D_STATE:(g + 1) * D_STATE]
        cg = cm[:, g * D_STATE:(g + 1) * D_STATE]
        cb = _nt_dot(cg, bg)
        y_parts = []
        for pp in range(pairs_per_group):
            hp = g * pairs_per_group + pp
            ha, hb = 2 * hp, 2 * hp + 1
            sl = slice(hp * LANES, (hp + 1) * LANES)
            xp = xc[:, sl]
            xdt = xp * jnp.where(lo, dt[:, ha:ha + 1], dt[:, hb:hb + 1])
            xdt_b = xdt.astype(BF16)
            seg_a = cum[:, ha:ha + 1] - cum_t[ha:ha + 1, :]
            seg_b = cum[:, hb:hb + 1] - cum_t[hb:hb + 1, :]
            m_a = (cb * jnp.exp(jnp.where(causal, seg_a, NEG_INF))).astype(BF16)
            m_b = (cb * jnp.exp(jnp.where(causal, seg_b, NEG_INF))).astype(BF16)
            y_a = jnp.dot(m_a, xdt_b, preferred_element_type=F32)
            y_b = jnp.dot(m_b, xdt_b, preferred_element_type=F32)
            y = jnp.where(lo, y_a, y_b)
            sp = s_ref[hp]
            cs = _nt_dot(cg, sp.astype(BF16))
            y = y + cs * jnp.where(lo, ecum[:, ha:ha + 1], ecum[:, hb:hb + 1])
            xw = (xdt * jnp.where(lo, wend[:, ha:ha + 1], wend[:, hb:hb + 1])).astype(BF16)
            upd = _tn_dot(xw, bg)
            dec = jnp.where(row_lo, ecl[:, ha:ha + 1], ecl[:, hb:hb + 1])
            s_ref[hp] = sp * dec + upd
            y = y + dskip_ref[:, sl] * xp
            zz = z_ref[:, sl]
            y = y * (zz * jax.nn.sigmoid(zz))
            y_parts.append(y)
        yg = jnp.concatenate(y_parts, axis=1)
        gw = D_INNER // SSD_GROUPS
        y_ref[:, g * gw:(g + 1) * gw] = _rms(yg, gn_ref[:, g * gw:(g + 1) * gw])

    @pl.when(c == nc - 1)
    def _():
        ssm_out_ref[0] = s_ref[...]
        tail_ref[0] = tail


def _ssd(y_in, cw, cb, dtb, alog, dskip, gn, conv0, ssm0, prev, layer, *, nb, q, nc, row0):
    rb = row0 // q
    half = SSD_HEADS // 2
    in_specs = [
        pl.BlockSpec((q, CONV_DIM), lambda b, c: (rb + b * nc + c, COL_XBC // CONV_DIM)),
        pl.BlockSpec((q, D_INNER), lambda b, c: (rb + b * nc + c, COL_Z // D_INNER)),
        pl.BlockSpec((q, LANES), lambda b, c: (rb + b * nc + c, COL_DT // LANES)),
        pl.BlockSpec((CONV_W, CONV_DIM), lambda b, c: (0, 0)),
        pl.BlockSpec((1, CONV_DIM), lambda b, c: (0, 0)),
        pl.BlockSpec((1, LANES), lambda b, c: (0, 0)),
        pl.BlockSpec((1, LANES), lambda b, c: (0, 0)),
        pl.BlockSpec((1, D_INNER), lambda b, c: (0, 0)),
        pl.BlockSpec((1, D_INNER), lambda b, c: (0, 0)),
    ]
    args = [y_in, y_in, y_in, cw, cb, dtb, alog, dskip, gn]
    aliases = {}
    has_init = conv0 is not None
    if has_init:
        in_specs += [
            pl.BlockSpec((None, 1, SUBLANES, CONV_DIM), lambda b, c: (layer, b, 0, 0)),
            pl.BlockSpec((None, 1, half, LANES, D_STATE), lambda b, c: (layer, b, 0, 0, 0)),
            pl.BlockSpec(memory_space=pl.ANY),
        ]
        args += [conv0, ssm0, prev]
        aliases = {len(args) - 1: 0}
    return pl.pallas_call(
        functools.partial(_ssd_kernel, q=q, nc=nc, has_init=has_init),
        grid=(nb, nc),
        in_specs=in_specs,
        out_specs=[
            pl.BlockSpec((q, D_INNER), lambda b, c: (rb + b * nc + c, 0)),
            pl.BlockSpec((1, half, LANES, D_STATE), lambda b, c: (b, 0, 0, 0)),
            pl.BlockSpec((1, SUBLANES, CONV_DIM), lambda b, c: (b, 0, 0)),
        ],
        out_shape=[
            jax.ShapeDtypeStruct((T_ALL, D_INNER), F32),
            jax.ShapeDtypeStruct((nb, half, LANES, D_STATE), F32),
            jax.ShapeDtypeStruct((nb, SUBLANES, CONV_DIM), F32),
        ],
        scratch_shapes=[
            pltpu.VMEM((q + SUBLANES, CONV_DIM), F32),
            pltpu.VMEM((half, LANES, D_STATE), F32),
        ],
        input_output_aliases=aliases,
        compiler_params=_cparams(("parallel", "arbitrary")),
        name="ssd_q%d" % q,
    )(*args)


def _half_rmsnorm(x, gain, lo):
    sq = x * x
    ms_lo = jnp.sum(jnp.where(lo, sq, 0.0), axis=-1, keepdims=True) * (1.0 / SWA_HEAD_DIM)
    ms_hi = jnp.sum(jnp.where(lo, 0.0, sq), axis=-1, keepdims=True) * (1.0 / SWA_HEAD_DIM)
    return x * jnp.where(lo, lax.rsqrt(ms_lo + EPS), lax.rsqrt(ms_hi + EPS)) * gain


def _both_halves(x, lo, take_lo):
    r = pltpu.roll(x, SWA_HEAD_DIM, axis=1)
    return jnp.where(lo, x, r) if take_lo else jnp.where(lo, r, x)


def _pad_rows(x, rows):
    if x.shape[0] == rows:
        return x
    return jnp.concatenate([x, jnp.zeros((rows - x.shape[0], x.shape[1]), x.dtype)], axis=0)


def _swa_kernel(*refs, lq, cached):
    if cached:
        sink_ref, slope_ref, q_ref, kc_ref, vc_ref, kp_ref, vp_ref, gq_ref, gk_ref, _, o_ref, kn_ref = refs
        kp, vp = kp_ref[0], vp_ref[0]
    else:
        sink_ref, slope_ref, q_ref, kc_ref, vc_ref, kp_ref, vp_ref, gq_ref, gk_ref, o_ref, kn_ref = refs
        kp, vp = kp_ref[...], vp_ref[...]
    gq_heads = SWA_HEADS // SWA_KV_HEADS
    rows = gq_heads * lq
    lo_q = lax.broadcasted_iota(jnp.int32, (lq, LANES), 1) < SWA_HEAD_DIM
    lo_w = lax.broadcasted_iota(jnp.int32, (WINDOW, LANES), 1) < SWA_HEAD_DIM
    gq = gq_ref[...]
    gk = gk_ref[...]

    kc = kc_ref[...]
    vc = vc_ref[...]
    kn_slabs, kp_slabs = [], []
    for s in range(2):
        sl = slice(s * LANES, (s + 1) * LANES)
        kn_s = _half_rmsnorm(kc[:, sl], gk, lo_q)
        kn_ref[:, sl] = kn_s
        kn_slabs.append(kn_s)
        kp_slabs.append(kp[:, sl] if cached else _half_rmsnorm(kp[:, sl], gk, lo_w))

    r_idx = lax.broadcasted_iota(jnp.int32, (rows, WINDOW), 0)
    j_idx = lax.broadcasted_iota(jnp.int32, (rows, WINDOW), 1)
    t_idx = r_idx % lq
    mask_prev = j_idx >= t_idx
    if not cached:
        mask_prev = jnp.logical_and(mask_prev, pl.program_id(1) > 0)
    mask_cur = j_idx <= t_idx
    dist_prev = (t_idx + WINDOW - j_idx).astype(F32)
    dist_cur = (t_idx - j_idx).astype(F32)
    head_of_row = lax.broadcasted_iota(jnp.int32, (rows, 1), 0) // lq
    scale = SWA_HEAD_DIM ** -0.5

    for kvh in range(SWA_KV_HEADS):
        slab, take_lo = kvh // 2, (kvh % 2 == 0)
        sl = slice(slab * LANES, (slab + 1) * LANES)
        k_cur = _pad_rows(_both_halves(kn_slabs[slab], lo_q, take_lo), WINDOW).astype(BF16)
        v_cur = _pad_rows(_both_halves(vc[:, sl], lo_q, take_lo), WINDOW).astype(BF16)
        k_prev = _both_halves(kp_slabs[slab], lo_w, take_lo).astype(BF16)
        v_prev = _both_halves(vp[:, sl], lo_w, take_lo).astype(BF16)

        q_parts = []
        for p in range(gq_heads // 2):
            c0 = kvh * gq_heads * SWA_HEAD_DIM + p * LANES
            qn = _half_rmsnorm(q_ref[:, c0:c0 + LANES], gq, lo_q)
            q_parts.append(jnp.where(lo_q, qn, 0.0))
            q_parts.append(jnp.where(lo_q, 0.0, qn))
        qs = jnp.concatenate(q_parts, axis=0).astype(BF16)

        sink = jnp.zeros((rows, 1), F32)
        slope = jnp.zeros((rows, 1), F32)
        for gi in range(gq_heads):
            sink = jnp.where(head_of_row == gi, sink_ref[kvh * gq_heads + gi], sink)
            slope = jnp.where(head_of_row == gi, slope_ref[kvh * gq_heads + gi], slope)

        s_prev = _nt_dot(qs, k_prev) * scale - slope * dist_prev
        s_cur = _nt_dot(qs, k_cur) * scale - slope * dist_cur
        s_prev = jnp.where(mask_prev, s_prev, NEG_INF)
        s_cur = jnp.where(mask_cur, s_cur, NEG_INF)
        m = jnp.maximum(jnp.maximum(jnp.max(s_prev, axis=-1, keepdims=True),
                                    jnp.max(s_cur, axis=-1, keepdims=True)), sink)
        p_prev = jnp.exp(s_prev - m)
        p_cur = jnp.exp(s_cur - m)
        den = (jnp.sum(p_prev, axis=-1, keepdims=True) + jnp.sum(p_cur, axis=-1, keepdims=True)
               + jnp.exp(sink - m))
        o = (jnp.dot(p_prev.astype(BF16), v_prev, preferred_element_type=F32)
             + jnp.dot(p_cur.astype(BF16), v_cur, preferred_element_type=F32)) / den
        for p in range(gq_heads // 2):
            c0 = kvh * gq_heads * SWA_HEAD_DIM + p * LANES
            o_lo = o[(2 * p) * lq:(2 * p + 1) * lq]
            o_hi = o[(2 * p + 1) * lq:(2 * p + 2) * lq]
            o_ref[:, c0:c0 + LANES] = jnp.where(lo_q, o_lo, o_hi)


def _swa(y_in, sinks, slopes, gq2, gk2, win_k, win_v, prev, layer, *, nb, lq, nq, row0):
    rb = row0 // lq
    kw = SWA_KV_HEADS * SWA_HEAD_DIM
    cached = win_k is not None
    in_specs = [
        pl.BlockSpec(memory_space=pltpu.SMEM),
        pl.BlockSpec(memory_space=pltpu.SMEM),
        pl.BlockSpec((lq, D_MODEL), lambda b, i: (rb + b * nq + i, COL_Q // D_MODEL)),
        pl.BlockSpec((lq, kw), lambda b, i: (rb + b * nq + i, COL_K // kw)),
        pl.BlockSpec((lq, kw), lambda b, i: (rb + b * nq + i, COL_V // kw)),
    ]
    args = [sinks, slopes, y_in, y_in, y_in]
    if cached:
        in_specs += [
            pl.BlockSpec((None, 1, WINDOW, kw), lambda b, i: (layer, b, 0, 0)),
            pl.BlockSpec((None, 1, WINDOW, kw), lambda b, i: (layer, b, 0, 0)),
        ]
        args += [win_k, win_v]
    else:
        in_specs += [
            pl.BlockSpec((lq, kw), lambda b, i: (rb + b * nq + jnp.maximum(i - 1, 0), COL_K // kw)),
            pl.BlockSpec((lq, kw), lambda b, i: (rb + b * nq + jnp.maximum(i - 1, 0), COL_V // kw)),
        ]
        args += [y_in, y_in]
    in_specs += [pl.BlockSpec((1, LANES), lambda b, i: (0, 0)), pl.BlockSpec((1, LANES), lambda b, i: (0, 0))]
    args += [gq2, gk2]
    aliases = {}
    if cached:
        in_specs.append(pl.BlockSpec(memory_space=pl.ANY))
        args.append(prev)
        aliases = {len(args) - 1: 0}
    return pl.pallas_call(
        functools.partial(_swa_kernel, lq=lq, cached=cached),
        grid=(nb, nq),
        in_specs=in_specs,
        out_specs=[
            pl.BlockSpec((lq, D_MODEL), lambda b, i: (rb + b * nq + i, 0)),
            pl.BlockSpec((lq, kw), lambda b, i: (b * nq + i, 0)),
        ],
        out_shape=[
            jax.ShapeDtypeStruct((T_ALL, D_MODEL), F32),
            jax.ShapeDtypeStruct((nb * nq * lq, kw), F32),
        ],
        input_output_aliases=aliases,
        compiler_params=_cparams(("parallel", "arbitrary")),
        name="swa_lq%d" % lq,
    )(*args)


def _mem_attn_kernel(*refs, cached):
    if cached:
        q_ref, k_ref, v_ref, _, o_ref = refs
    else:
        q_ref, k_ref, v_ref, o_ref = refs
    scale = MEM_HEAD_DIM ** -0.5
    for h in range(MEM_HEADS):
        sl = slice(h * MEM_HEAD_DIM, (h + 1) * MEM_HEAD_DIM)
        qh = q_ref[:, sl].astype(BF16)
        if cached:
            kh = k_ref[0, :, sl].astype(BF16)
            vh = v_ref[0, :, sl].astype(BF16)
        else:
            kh = k_ref[:, sl].astype(BF16)
            vh = v_ref[:, sl].astype(BF16)
        s = _nt_dot(qh, kh) * scale
        m = jnp.max(s, axis=-1, keepdims=True)
        p = jnp.exp(s - m)
        den = jnp.sum(p, axis=-1, keepdims=True)
        o_ref[:, sl] = jnp.dot(p.astype(BF16), vh, preferred_element_type=F32) / den


def _mem_attn(y_in, mem_k, mem_v, prev, layer, *, nb, tq, nq, row0):
    rb = row0 // tq
    cached = prev is not None
    in_specs = [pl.BlockSpec((tq, D_MODEL), lambda b, i: (rb + b * nq + i, COL_QM // D_MODEL))]
    args = [y_in, mem_k, mem_v]
    aliases = {}
    if cached:
        in_specs += [
            pl.BlockSpec((None, 1, N_MEM, D_MODEL), lambda b, i: (layer, b, 0, 0)),
            pl.BlockSpec((None, 1, N_MEM, D_MODEL), lambda b, i: (layer, b, 0, 0)),
            pl.BlockSpec(memory_space=pl.ANY),
        ]
        args.append(prev)
        aliases = {len(args) - 1: 0}
    else:
        in_specs += [
            pl.BlockSpec((N_MEM, D_MODEL), lambda b, i: (b, 0)),
            pl.BlockSpec((N_MEM, D_MODEL), lambda b, i: (b, 1)),
        ]
    return pl.pallas_call(
        functools.partial(_mem_attn_kernel, cached=cached),
        grid=(nb, nq),
        in_specs=in_specs,
        out_specs=pl.BlockSpec((tq, D_MODEL), lambda b, i: (rb + b * nq + i, 0)),
        out_shape=jax.ShapeDtypeStruct((T_ALL, D_MODEL), F32),
        input_output_aliases=aliases,
        compiler_params=_cparams(("parallel", "arbitrary")),
        name="mem_attn_tq%d" % tq,
    )(*args)


def _merge_kernel(a0_ref, a1_ref, a2_ref, w0_ref, w1_ref, w2_ref, g0_ref, g1_ref, g2_ref,
                  b0_ref, b1_ref, b2_ref, o_ref, ab_ref):
    @pl.when(pl.program_id(1) == 0)
    def _():
        for k, a_ref in enumerate((a0_ref, a1_ref, a2_ref)):
            ab_ref[k] = a_ref[...].astype(BF16)

    acc = None
    for k, (w_ref, g_ref, b_ref) in enumerate(((w0_ref, g0_ref, b0_ref), (w1_ref, g1_ref, b1_ref),
                                               (w2_ref, g2_ref, b2_ref))):
        br = jnp.dot(ab_ref[k], w_ref[...], preferred_element_type=F32)
        t = jax.nn.sigmoid(g_ref[...] + b_ref[...]) * br
        acc = t if acc is None else acc + t
    o_ref[...] = acc.astype(BF16)


def _merge(a0, a1, a2, w0, w1, w2, y_in, b_gate, layer, *, tm, tn):
    gb = COL_GATE // tn
    nbk = D_MODEL // tn
    a_spec = pl.BlockSpec((tm, D_MODEL), lambda i, j: (i, 0), pipeline_mode=pl.Buffered(1))
    w_spec = pl.BlockSpec((None, D_MODEL, tn), lambda i, j: (layer, 0, j))
    return pl.pallas_call(
        _merge_kernel,
        grid=(T_ALL // tm, nbk),
        in_specs=[a_spec, a_spec, a_spec, w_spec, w_spec, w_spec,
                  pl.BlockSpec((tm, tn), lambda i, j: (i, gb + j)),
                  pl.BlockSpec((tm, tn), lambda i, j: (i, gb + nbk + j)),
                  pl.BlockSpec((tm, tn), lambda i, j: (i, gb + 2 * nbk + j)),
                  pl.BlockSpec((1, tn), lambda i, j: (0, j)),
                  pl.BlockSpec((1, tn), lambda i, j: (0, nbk + j)),
                  pl.BlockSpec((1, tn), lambda i, j: (0, 2 * nbk + j))],
        out_specs=pl.BlockSpec((tm, tn), lambda i, j: (i, j)),
        out_shape=jax.ShapeDtypeStruct((T_ALL, D_MODEL), BF16),
        scratch_shapes=[pltpu.VMEM((N_BRANCH, tm, D_MODEL), BF16)],
        compiler_params=_cparams(("parallel", "arbitrary")),
        name="gated_merge",
    )(a0, a1, a2, w0, w1, w2, y_in, y_in, y_in, b_gate, b_gate, b_gate)


def _matmul_res_kernel(a_ref, w_ref, r_ref, o_ref):
    o_ref[...] = r_ref[...] + jnp.dot(a_ref[...], w_ref[...], preferred_element_type=F32)


def _matmul_res(a, w, r, layer, *, tn):
    m, k = a.shape
    n = w.shape[2]
    return pl.pallas_call(
        _matmul_res_kernel,
        grid=(m // TM, n // tn),
        in_specs=[pl.BlockSpec((TM, k), lambda i, j: (i, 0)),
                  pl.BlockSpec((None, k, tn), lambda i, j: (layer, 0, j)),
                  pl.BlockSpec((TM, tn), lambda i, j: (i, j))],
        out_specs=pl.BlockSpec((TM, tn), lambda i, j: (i, j)),
        out_shape=jax.ShapeDtypeStruct((m, n), F32),
        compiler_params=_cparams(("parallel", "arbitrary")),
        name="matmul_residual",
    )(a, w, r)


def _top_values(s, k):
    vals = []
    cur = s
    for r in range(k):
        mx = jnp.max(cur, axis=0, keepdims=True)
        vals.append(mx)
        if r + 1 < k:
            cur = jnp.where(cur == mx, NEG_INF, cur)
    return vals


def _router_kernel(x_ref, g_ref, wqt_ref, keys_ref, h2_ref, e0_ref, th_ref, e1_ref):
    h2 = _rms(x_ref[...], g_ref[...]).astype(BF16)
    h2_ref[...] = h2
    qt = _nt_dot(wqt_ref[...], h2)
    for h in range(PEER_HEADS):
        s0 = jnp.dot(keys_ref[2 * h], qt[(2 * h) * PEER_HALF:(2 * h + 1) * PEER_HALF, :].astype(BF16),
                     preferred_element_type=F32)
        s1 = jnp.dot(keys_ref[2 * h + 1], qt[(2 * h + 1) * PEER_HALF:(2 * h + 2) * PEER_HALF, :].astype(BF16),
                     preferred_element_type=F32)
        top0 = _top_values(s0, PEER_TOPK)
        top1 = _top_values(s1, PEER_TOPK)
        a1 = jnp.concatenate(top1, axis=0)
        half_k = PEER_TOPK // 2
        cand = jnp.concatenate(
            [top0[0] + a1]
            + [top0[r] + a1[:half_k] for r in range(1, half_k)]
            + [jnp.concatenate(top0[half_k:], axis=0) + top1[0]], axis=0)
        best = _top_values(cand, PEER_TOPK)
        m = best[0]
        tau = best[PEER_TOPK - 1]
        z = jnp.zeros_like(m)
        for r in range(PEER_TOPK):
            z = z + jnp.exp(best[r] - m)
        e1_sorted = jnp.exp(a1 - top1[0])
        th = jnp.full(s0.shape, NEVER, F32)
        for r in range(PEER_TOPK):
            th_r = jnp.min(jnp.where((top0[r] + a1) >= tau, e1_sorted, NEVER), axis=0, keepdims=True)
            th = jnp.where(s0 == top0[r], th_r, th)
        e0_ref[h] = 0.5 * jnp.exp(s0 - top0[0]) / z
        th_ref[h] = th
        e1_ref[h] = jnp.exp(s1 - top1[0])


def _router(x, g, wqt, keys, layer, *, tmr):
    t = x.shape[0]
    tok = pl.BlockSpec((PEER_HEADS, N_KEYS, tmr), lambda i: (0, 0, i))
    per_key = jax.ShapeDtypeStruct((PEER_HEADS, N_KEYS, t), F32)
    return pl.pallas_call(
        _router_kernel,
        grid=(t // tmr,),
        in_specs=[pl.BlockSpec((tmr, D_MODEL), lambda i: (i, 0)),
                  pl.BlockSpec((1, D_MODEL), lambda i: (0, 0)),
                  pl.BlockSpec((None, PEER_HEADS * PEER_QUERY_DIM, D_MODEL), lambda i: (layer, 0, 0)),
                  pl.BlockSpec((None, 2 * PEER_HEADS, N_KEYS, PEER_HALF), lambda i: (layer, 0, 0, 0))],
        out_specs=[pl.BlockSpec((tmr, D_MODEL), lambda i: (i, 0)), tok, tok, tok],
        out_shape=[jax.ShapeDtypeStruct((t, D_MODEL), BF16), per_key, per_key, per_key],
        compiler_params=_cparams(("parallel",)),
        name="peer_router",
    )(x, g, wqt, keys)


def _peer_dense_kernel(h2_ref, u_ref, vt_ref, e0_ref, th_ref, e1_ref, x_ref, o_ref,
                       acc_ref, at_ref, wa_ref, wb_ref, h2t_ref, *, te, ne, tm):
    e = pl.program_id(1)
    per_step = te // N_KEYS
    assert 2 * per_step == SUBLANES

    @pl.when(e == 0)
    def _():
        acc_ref[...] = jnp.zeros_like(acc_ref)
        wb_ref[...] = jnp.zeros_like(wb_ref)
        h2t_ref[...] = h2_ref[...].T

    chunk = 2 * LANES
    n_chunks = tm // chunk
    vrows = D_MODEL // (2 * per_step)

    def value_piece(w_old_ref, c, m):
        rows = slice(m * vrows, (m + 1) * vrows)
        cols = slice(c * chunk, (c + 1) * chunk)
        acc_ref[rows, cols] += jnp.dot(vt_ref[rows, :], w_old_ref[:, cols], preferred_element_type=F32)

    def step(w_new_ref, w_old_ref, parity):
        def score_piece(c, ii):
            rows = slice(ii * N_KEYS, (ii + 1) * N_KEYS)
            cols = slice(c * chunk, (c + 1) * chunk)
            at_ref[rows, cols] = jnp.dot(u_ref[rows, :], h2t_ref[:, cols], preferred_element_type=F32)

        def gate_cell(ii, tg):
            rows = slice(ii * N_KEYS, (ii + 1) * N_KEYS)
            cs = slice(tg * LANES, (tg + 1) * LANES)
            r = parity * per_step + ii
            gate = None
            for h in range(PEER_HEADS):
                e1 = e1_ref[h, :, cs]
                t = jnp.where(e1 >= th_ref[h, r:r + 1, cs], e1 * e0_ref[h, r:r + 1, cs], 0.0)
                gate = t if gate is None else gate + t
            a = at_ref[rows, cs]
            gelu2 = a * (1.0 + lax.erf(a * (2.0 ** -0.5)))
            w_new_ref[rows, cs] = (gate * gelu2).astype(BF16)

        rounds = [(c, ii) for c in range(n_chunks) for ii in range(per_step)]
        score_piece(*rounds[0])
        for k, (c, ii) in enumerate(rounds):
            if k + 1 < len(rounds):
                score_piece(*rounds[k + 1])
            gate_cell(ii, 2 * c)
            value_piece(w_old_ref, c, 2 * ii)
            gate_cell(ii, 2 * c + 1)
            value_piece(w_old_ref, c, 2 * ii + 1)

    @pl.when(jnp.logical_and(e % 2 == 0, e < ne))
    def _():
        step(wa_ref, wb_ref, 0)

    @pl.when(jnp.logical_and(e % 2 == 1, e < ne))
    def _():
        step(wb_ref, wa_ref, 1)

    @pl.when(e == ne)
    def _():
        w_last_ref = wb_ref if ne % 2 == 0 else wa_ref
        for c in range(n_chunks):
            for m in range(D_MODEL // vrows):
                value_piece(w_last_ref, c, m)
        o_ref[...] = x_ref[...] + acc_ref[...].T


def _peer_dense(h2, u, vt, e0, th, e1, x, layer, *, te, tm):
    t = x.shape[0]
    ne = N_EXPERTS // te
    once = pl.Buffered(1)
    by_first = pl.BlockSpec((PEER_HEADS, SUBLANES, tm), lambda i, e: (0, jnp.minimum(e, ne - 1) // 2, i))
    by_second = pl.BlockSpec((PEER_HEADS, N_KEYS, tm), lambda i, e: (0, 0, i), pipeline_mode=once)
    return pl.pallas_call(
        functools.partial(_peer_dense_kernel, te=te, ne=ne, tm=tm),
        grid=(t // tm, ne + 1),
        in_specs=[pl.BlockSpec((tm, D_MODEL), lambda i, e: (i, 0), pipeline_mode=once),
                  pl.BlockSpec((None, te, D_MODEL), lambda i, e: (layer, jnp.minimum(e, ne - 1), 0)),
                  pl.BlockSpec((None, D_MODEL, te), lambda i, e: (layer, 0, jnp.maximum(e - 1, 0))),
                  by_first, by_first, by_second,
                  pl.BlockSpec((tm, D_MODEL), lambda i, e: (i, 0), pipeline_mode=once)],
        out_specs=pl.BlockSpec((tm, D_MODEL), lambda i, e: (i, 0)),
        out_shape=jax.ShapeDtypeStruct((t, D_MODEL), F32),
        scratch_shapes=[pltpu.VMEM((D_MODEL, tm), F32), pltpu.VMEM((te, tm), F32),
                        pltpu.VMEM((te, tm), BF16), pltpu.VMEM((te, tm), BF16),
                        pltpu.VMEM((D_MODEL, tm), BF16)],
        compiler_params=_cparams(("parallel", "arbitrary")),
        name="peer_dense",
    )(h2, u, vt, e0, th, e1, x)


def _pad_lanes(v, n=LANES):
    return jnp.pad(v, (0, n - v.shape[0])).reshape(1, n)


def kernel(x_prompt, x_sample, cache_win_k, cache_win_v, state_ssm, state_conv, cache_mem_k, cache_mem_v,
           mem_prompt, w_in, b_gate, conv_w, conv_b, dt_bias, a_log, d_skip, g_ssd_norm, g_q, g_k,
           attn_sinks, g_mem, w_mem_kv, g_qm, g_km, w_o_ssm, w_o_swa, w_o_mem, w_out, g_mix, g_ffn,
           w_peer_q, peer_sub_keys, peer_u, peer_v):
    x = jnp.concatenate([x_prompt.reshape(T_PROMPT, D_MODEL), x_sample.reshape(T_SAMPLE, D_MODEL)], axis=0)
    slopes = jnp.exp2(-8.0 * jnp.arange(1, SWA_HEADS + 1, dtype=F32) / SWA_HEADS)
    mem_x = mem_prompt.reshape(BATCH * N_MEM, D_MODEL)
    kw = SWA_KV_HEADS * SWA_HEAD_DIM
    half = SSD_HEADS // 2
    row = lambda v: v.reshape(1, -1)

    w_in_t = jnp.swapaxes(w_in, 1, 2)
    w_mem_kv_b = w_mem_kv.astype(BF16)
    w_o_ssm_b, w_o_swa_b, w_o_mem_b = w_o_ssm.astype(BF16), w_o_swa.astype(BF16), w_o_mem.astype(BF16)
    w_out_b = w_out.astype(BF16)
    wqt_b = jnp.swapaxes(w_peer_q, 1, 2).astype(BF16)
    keys_b = peer_sub_keys.reshape(DEPTH, 2 * PEER_HEADS, N_KEYS, PEER_HALF).astype(BF16)
    u_b = peer_u.astype(BF16)
    vt_b = jnp.swapaxes(peer_v, 1, 2).astype(BF16)
    conv0 = jnp.pad(state_conv, ((0, 0), (0, 0), (SUBLANES - (CONV_W - 1), 0), (0, 0)))
    ssm0 = state_ssm.reshape(DEPTH, DEC_BATCH, half, LANES, D_STATE)
    wk_c = cache_win_k.reshape(DEPTH, DEC_BATCH, WINDOW, kw)
    wv_c = cache_win_v.reshape(DEPTH, DEC_BATCH, WINDOW, kw)
    mk_c = cache_mem_k.reshape(DEPTH, DEC_BATCH, N_MEM, D_MODEL)
    mv_c = cache_mem_v.reshape(DEPTH, DEC_BATCH, N_MEM, D_MODEL)

    outs = {k: [] for k in ("p_wk", "p_wv", "p_ssm", "p_conv", "p_mk", "p_mv", "s_wk", "s_wv", "s_ssm", "s_conv")}
    for l in range(DEPTH):
        mem_kv = _norm_matmul(mem_x, row(g_mem[l]), w_mem_kv_b, row(g_km[l]), l,
                              tm=512, tn=MEM_HEAD_DIM, norm_lo=0, norm_hi=MEM_HEADS)
        outs["p_mk"].append(mem_kv[:, :D_MODEL].reshape(BATCH, N_MEM, MEM_HEADS, MEM_HEAD_DIM))
        outs["p_mv"].append(mem_kv[:, D_MODEL:].reshape(BATCH, N_MEM, MEM_HEADS, MEM_HEAD_DIM))

        y = _in_proj(_rms_cast(x, row(g_mix[l])), w_in_t, row(g_qm[l]), l)

        cw, cb = conv_w[l], row(conv_b[l])
        dtb, alog = _pad_lanes(dt_bias[l]), _pad_lanes(a_log[l])
        dskip = row(jnp.repeat(d_skip[l], SSD_HEAD_DIM))
        gn = row(g_ssd_norm[l])
        y_ssm, p_ssm, p_tail = _ssd(y, cw, cb, dtb, alog, dskip, gn, None, None, None, l,
                                    nb=BATCH, q=SSD_CHUNK, nc=SEQ // SSD_CHUNK, row0=0)
        y_ssm, s_ssm, s_tail = _ssd(y, cw, cb, dtb, alog, dskip, gn, conv0, ssm0, y_ssm, l,
                                    nb=DEC_BATCH, q=DEC_SEQ, nc=1, row0=T_PROMPT)
        outs["p_ssm"].append(p_ssm.reshape(BATCH, SSD_HEADS, SSD_HEAD_DIM, D_STATE))
        outs["s_ssm"].append(s_ssm.reshape(DEC_BATCH, SSD_HEADS, SSD_HEAD_DIM, D_STATE))
        outs["p_conv"].append(p_tail[:, SUBLANES - (CONV_W - 1):])
        outs["s_conv"].append(s_tail[:, SUBLANES - (CONV_W - 1):])

        gq2 = row(jnp.concatenate([g_q[l], g_q[l]]))
        gk2 = row(jnp.concatenate([g_k[l], g_k[l]]))
        o_swa, kn_p = _swa(y, attn_sinks[l], slopes, gq2, gk2, None, None, None, l,
                           nb=BATCH, lq=WINDOW, nq=SEQ // WINDOW, row0=0)
        o_swa, kn_s = _swa(y, attn_sinks[l], slopes, gq2, gk2, wk_c, wv_c, o_swa, l,
                           nb=DEC_BATCH, lq=DEC_SEQ, nq=1, row0=T_PROMPT)
        v_p = y[:T_PROMPT, COL_V:COL_V + kw].reshape(BATCH, SEQ, kw)
        v_s = y[T_PROMPT:, COL_V:COL_V + kw].reshape(DEC_BATCH, DEC_SEQ, kw)
        kv_shape = (SWA_KV_HEADS, SWA_HEAD_DIM)
        outs["p_wk"].append(kn_p.reshape(BATCH, SEQ, kw)[:, SEQ - WINDOW:].reshape(BATCH, WINDOW, *kv_shape))
        outs["p_wv"].append(v_p[:, SEQ - WINDOW:].reshape(BATCH, WINDOW, *kv_shape))
        outs["s_wk"].append(jnp.concatenate([wk_c[l][:, DEC_SEQ:], kn_s.reshape(DEC_BATCH, DEC_SEQ, kw)], axis=1)
                            .reshape(DEC_BATCH, WINDOW, *kv_shape))
        outs["s_wv"].append(jnp.concatenate([wv_c[l][:, DEC_SEQ:], v_s], axis=1)
                            .reshape(DEC_BATCH, WINDOW, *kv_shape))

        o_mem = _mem_attn(y, mem_kv, mem_kv, None, l, nb=BATCH, tq=512, nq=SEQ // 512, row0=0)
        o_mem = _mem_attn(y, mk_c, mv_c, o_mem, l, nb=DEC_BATCH, tq=DEC_SEQ, nq=1, row0=T_PROMPT)

        merged = _merge(y_ssm, o_swa, o_mem, w_o_ssm_b, w_o_swa_b, w_o_mem_b, y, row(b_gate[l]), l,
                        tm=TM, tn=512)
        x = _matmul_res(merged, w_out_b, x, l, tn=512)

        h2, e0, th, e1 = _router(x, row(g_ffn[l]), wqt_b, keys_b, l, tmr=256)
        x = _peer_dense(h2, u_b, vt_b, e0, th, e1, x, l, te=512, tm=TM)

    yp = x[:T_PROMPT].reshape(BATCH, SEQ, D_MODEL)
    ys = x[T_PROMPT:].reshape(DEC_BATCH, DEC_SEQ, D_MODEL)
    st = lambda k: jnp.stack(outs[k])
    return (yp, ys, st("p_wk"), st("p_wv"), st("p_ssm"), st("p_conv"), st("p_mk"), st("p_mv"),
            st("s_wk"), st("s_wv"), st("s_ssm"), st("s_conv"))
```

```python
import functools

import jax
import jax.numpy as jnp
from jax import lax
from jax.experimental import pallas as pl
from jax.experimental.pallas import tpu as pltpu

F32 = jnp.float32
BF16 = jnp.bfloat16

D_MODEL = 2048
BATCH = 4
SEQ = 2048
DEPTH = 2
DEC_BATCH = 32
DEC_SEQ = 8
D_INNER = D_MODEL
SSD_HEAD_DIM = 64
SSD_HEADS = D_INNER // SSD_HEAD_DIM
SSD_GROUPS = 4
D_STATE = 128
CONV_W = 4
CONV_DIM = D_INNER + 2 * SSD_GROUPS * D_STATE
SSD_CHUNK = 128
SWA_HEAD_DIM = 64
SWA_HEADS = D_MODEL // SWA_HEAD_DIM
SWA_KV_HEADS = SWA_HEADS // 8
WINDOW = 128
N_MEM = 256
MEM_HEADS = 4
MEM_HEAD_DIM = D_MODEL // MEM_HEADS
PEER_HEADS = 8
N_KEYS = 128
N_EXPERTS = N_KEYS * N_KEYS
PEER_QUERY_DIM = 256
PEER_HALF = PEER_QUERY_DIM // 2
PEER_TOPK = 16
N_BRANCH = 3
EPS = 1e-6

LANES = 128
SUBLANES = 8
VMEM_LIMIT = 56 * 1024 * 1024

T_PROMPT = BATCH * SEQ
T_SAMPLE = DEC_BATCH * DEC_SEQ
T_ALL = T_PROMPT + T_SAMPLE
TM = 768

COL_XBC = 0
COL_K = CONV_DIM
COL_V = COL_K + 256
COL_DT = COL_V + 256
COL_Z = 4096
COL_Q = COL_Z + D_INNER
COL_QM = COL_Q + D_MODEL
COL_GATE = COL_QM + D_MODEL
IN_PAD = COL_GATE + N_BRANCH * D_MODEL

NEG_INF = float("-inf")
NEVER = 2.0


def _cparams(sem, flags=None):
    return pltpu.CompilerParams(dimension_semantics=sem, vmem_limit_bytes=VMEM_LIMIT, flags=flags)


def _nt_dot(a, b):
    return lax.dot_general(a, b, (((1,), (1,)), ((), ())), preferred_element_type=F32)


def _tn_dot(a, b):
    return lax.dot_general(a, b, (((0,), (0,)), ((), ())), preferred_element_type=F32)


def _mixer_out_shape(carried, rows):
    if carried:
        return jax.ShapeDtypeStruct((rows, D_MODEL), F32)
    return jax.ShapeDtypeStruct((T_ALL, D_MODEL), BF16)


def _rms(y, gain):
    ms = jnp.mean(y * y, axis=-1, keepdims=True)
    return y * lax.rsqrt(ms + EPS) * gain


def _norm_matmul_kernel(x_ref, g_ref, w_ref, gn_ref, o_ref, h_ref, *, norm_lo, norm_hi):
    n = pl.program_id(1)

    @pl.when(n == 0)
    def _():
        h_ref[...] = _rms(x_ref[...], g_ref[...]).astype(BF16)

    y = jnp.dot(h_ref[...], w_ref[...], preferred_element_type=F32)
    is_norm = jnp.logical_and(n >= norm_lo, n < norm_hi)

    @pl.when(is_norm)
    def _():
        for c in range(y.shape[1] // MEM_HEAD_DIM):
            sl = slice(c * MEM_HEAD_DIM, (c + 1) * MEM_HEAD_DIM)
            o_ref[:, sl] = _rms(y[:, sl], gn_ref[...])

    @pl.when(jnp.logical_not(is_norm))
    def _():
        o_ref[...] = y


def _norm_matmul(x, g, w, gn, layer, *, tm, tn, norm_lo, norm_hi):
    m, k = x.shape
    n = w.shape[2]
    return pl.pallas_call(
        functools.partial(_norm_matmul_kernel, norm_lo=norm_lo, norm_hi=norm_hi),
        grid=(m // tm, n // tn),
        in_specs=[
            pl.BlockSpec((tm, k), lambda i, j: (i, 0)),
            pl.BlockSpec((1, k), lambda i, j: (0, 0)),
            pl.BlockSpec((None, k, tn), lambda i, j: (layer, 0, j)),
            pl.BlockSpec((1, MEM_HEAD_DIM), lambda i, j: (0, 0)),
        ],
        out_specs=pl.BlockSpec((tm, tn), lambda i, j: (i, j)),
        out_shape=jax.ShapeDtypeStruct((m, n), F32),
        scratch_shapes=[pltpu.VMEM((tm, k), BF16)],
        compiler_params=_cparams(("parallel", "arbitrary")),
        name="norm_matmul",
    )(x, g, w, gn)


IN_UNIT = 256
IN_TILE = 1024


def _w_in_row_map():
    z0, xbc0 = 0, D_INNER
    dt0 = xbc0 + CONV_DIM
    q0 = dt0 + SSD_HEADS
    k0 = q0 + D_MODEL
    v0 = k0 + 256
    qm0 = v0 + 256
    gate0 = qm0 + D_MODEL
    rows = {}
    for dst, src, width in ((COL_XBC, xbc0, CONV_DIM), (COL_K, k0, 256), (COL_V, v0, 256),
                            (COL_DT, dt0, IN_UNIT), (COL_DT + IN_UNIT, dt0, IN_UNIT), (COL_Z, z0, D_INNER),
                            (COL_Q, q0, D_MODEL), (COL_QM, qm0, D_MODEL), (COL_GATE, gate0, N_BRANCH * D_MODEL)):
        for u in range(width // IN_UNIT):
            rows[dst // IN_UNIT + u] = src + u * IN_UNIT
    return [rows[d] for d in range(IN_PAD // IN_UNIT)]


def _rms_cast_kernel(x_ref, g_ref, o_ref):
    o_ref[...] = _rms(x_ref[...], g_ref[...]).astype(BF16)


def _rms_cast(x, g):
    m, k = x.shape
    return pl.pallas_call(
        _rms_cast_kernel,
        grid=(m // TM,),
        in_specs=[pl.BlockSpec((TM, k), lambda i: (i, 0)), pl.BlockSpec((1, k), lambda i: (0, 0))],
        out_specs=pl.BlockSpec((TM, k), lambda i: (i, 0)),
        out_shape=jax.ShapeDtypeStruct((m, k), BF16),
        compiler_params=_cparams(("parallel",)),
        name="rms_cast",
    )(x, g)


def _in_proj_kernel(rows_ref, h_ref, w0_ref, w1_ref, w2_ref, w3_ref, gn_ref, o_ref, wb_ref, *, norm_lo, norm_hi):
    del rows_ref
    n = pl.program_id(0)

    @pl.when(pl.program_id(1) == 0)
    def _():
        for u, w_ref in enumerate((w0_ref, w1_ref, w2_ref, w3_ref)):
            wb_ref[u * IN_UNIT:(u + 1) * IN_UNIT, :] = w_ref[...].astype(BF16)

    y = _nt_dot(h_ref[...], wb_ref[...])
    is_norm = jnp.logical_and(n >= norm_lo, n < norm_hi)

    @pl.when(is_norm)
    def _():
        for c in range(IN_TILE // MEM_HEAD_DIM):
            sl = slice(c * MEM_HEAD_DIM, (c + 1) * MEM_HEAD_DIM)
            o_ref[:, sl] = _rms(y[:, sl], gn_ref[...])

    @pl.when(jnp.logical_not(is_norm))
    def _():
        o_ref[...] = y


def _in_proj(h, w_t, gn, layer):
    t = h.shape[0]
    per_tile = IN_TILE // IN_UNIT

    def w_spec(u):
        return pl.BlockSpec((None, pl.Element(IN_UNIT), pl.Element(D_MODEL)),
                            lambda n, m, rows: (layer, pl.multiple_of(rows[per_tile * n + u], SSD_HEADS), 0))

    grid_spec = pltpu.PrefetchScalarGridSpec(
        num_scalar_prefetch=1,
        grid=(IN_PAD // IN_TILE, t // TM),
        in_specs=[pl.BlockSpec((TM, D_MODEL), lambda n, m, rows: (m, 0))]
        + [w_spec(u) for u in range(per_tile)]
        + [pl.BlockSpec((1, MEM_HEAD_DIM), lambda n, m, rows: (0, 0))],
        out_specs=pl.BlockSpec((TM, IN_TILE), lambda n, m, rows: (m, n)),
        scratch_shapes=[pltpu.VMEM((IN_TILE, D_MODEL), BF16)],
    )
    return pl.pallas_call(
        functools.partial(_in_proj_kernel, norm_lo=COL_QM // IN_TILE, norm_hi=COL_GATE // IN_TILE),
        grid_spec=grid_spec,
        out_shape=jax.ShapeDtypeStruct((t, IN_PAD), F32),
        compiler_params=_cparams(("parallel", "arbitrary")),
        name="in_proj",
    )(jnp.asarray(_w_in_row_map(), jnp.int32), h, w_t, w_t, w_t, w_t, gn)


def _ssd_kernel(*refs, q, nc, has_init):
    if has_init:
        (xbc_ref, z_ref, dt_ref, cw_ref, cb_ref, dtb_ref, alog_ref, dskip_ref, gn_ref, conv0_ref, ssm0_ref,
         y_ref, ssm_out_ref, tail_ref, ext_ref, s_ref) = refs
    else:
        (xbc_ref, z_ref, dt_ref, cw_ref, cb_ref, dtb_ref, alog_ref, dskip_ref, gn_ref,
         y_ref, ssm_out_ref, tail_ref, ext_ref, s_ref) = refs
    c = pl.program_id(1)

    @pl.when(c == 0)
    def _():
        if has_init:
            ext_ref[0:SUBLANES, :] = conv0_ref[0]
            s_ref[...] = ssm0_ref[0]
        else:
            ext_ref[0:SUBLANES, :] = jnp.zeros((SUBLANES, CONV_DIM), F32)
            s_ref[...] = jnp.zeros_like(s_ref)

    ext_ref[SUBLANES:SUBLANES + q, :] = xbc_ref[...]
    w = cw_ref[...]
    conv = (cb_ref[...]
            + w[3:4] * ext_ref[SUBLANES:SUBLANES + q, :]
            + w[2:3] * ext_ref[SUBLANES - 1:SUBLANES - 1 + q, :]
            + w[1:2] * ext_ref[SUBLANES - 2:SUBLANES - 2 + q, :]
            + w[0:1] * ext_ref[SUBLANES - 3:SUBLANES - 3 + q, :])
    tail = ext_ref[q:q + SUBLANES, :]
    ext_ref[0:SUBLANES, :] = tail
    xc = conv * jax.nn.sigmoid(conv)

    gn_w = SSD_GROUPS * D_STATE
    bm = xc[:, D_INNER:D_INNER + gn_w].astype(BF16)
    cm = xc[:, D_INNER + gn_w:].astype(BF16)

    dtr = dt_ref[...] + dtb_ref[...]
    dt = jnp.maximum(dtr, 0.0) + jnp.log1p(jnp.exp(-jnp.abs(dtr)))
    a_neg = -jnp.exp(alog_ref[...])
    da = dt * a_neg
    row = lax.broadcasted_iota(jnp.int32, (q, q), 0)
    col = lax.broadcasted_iota(jnp.int32, (q, q), 1)
    causal = row >= col
    tri = causal.astype(F32)
    cum = jnp.dot(tri, da, preferred_element_type=F32, precision=lax.Precision.HIGHEST)
    if q < LANES:
        cum_sq = jnp.concatenate([cum, jnp.zeros((LANES - q, LANES), F32)], axis=0)
    else:
        cum_sq = cum
    cum_t = cum_sq.T[:, :q]
    ecum = jnp.exp(cum)
    cl = cum[q - 1:q, :]
    wend = jnp.exp(cl - cum)
    ecl = jnp.exp(cl)

    lo = lax.broadcasted_iota(jnp.int32, (q, LANES), 1) < SSD_HEAD_DIM
    row_lo = lax.broadcasted_iota(jnp.int32, (LANES, LANES), 0) < SSD_HEAD_DIM
    pairs_per_group = SSD_HEADS // SSD_GROUPS // 2

    for g in range(SSD_GROUPS):
        bg = bm[:, g * D_STATE:(g + 1) * D_STATE]
        cg = cm[:, g * D_STATE:(g + 1) * D_STATE]
        cb = _nt_dot(cg, bg)
        y_parts = []
        for pp in range(pairs_per_group):
            hp = g * pairs_per_group + pp
            ha, hb = 2 * hp, 2 * hp + 1
            sl = slice(hp * LANES, (hp + 1) * LANES)
            xp = xc[:, sl]
            xdt = xp * jnp.where(lo, dt[:, ha:ha + 1], dt[:, hb:hb + 1])
            xdt_b = xdt.astype(BF16)
            seg_a = cum[:, ha:ha + 1] - cum_t[ha:ha + 1, :]
            seg_b = cum[:, hb:hb + 1] - cum_t[hb:hb + 1, :]
            m_a = (cb * jnp.exp(jnp.where(causal, seg_a, NEG_INF))).astype(BF16)
            m_b = (cb * jnp.exp(jnp.where(causal, seg_b, NEG_INF))).astype(BF16)
            y_a = jnp.dot(m_a, xdt_b, preferred_element_type=F32)
            y_b = jnp.dot(m_b, xdt_b, preferred_element_type=F32)
            y = jnp.where(lo, y_a, y_b)
            sp = s_ref[hp]
            cs = _nt_dot(cg, sp.astype(BF16))
            y = y + cs * jnp.where(lo, ecum[:, ha:ha + 1], ecum[:, hb:hb + 1])
            xw = (xdt * jnp.where(lo, wend[:, ha:ha + 1], wend[:, hb:hb + 1])).astype(BF16)
            upd = _tn_dot(xw, bg)
            dec = jnp.where(row_lo, ecl[:, ha:ha + 1], ecl[:, hb:hb + 1])
            s_ref[hp] = sp * dec + upd
            y = y + dskip_ref[:, sl] * xp
            zz = z_ref[:, sl]
            y = y * (zz * jax.nn.sigmoid(zz))
            y_parts.append(y)
        yg = jnp.concatenate(y_parts, axis=1)
        gw = D_INNER // SSD_GROUPS
        y_ref[:, g * gw:(g + 1) * gw] = _rms(yg, gn_ref[:, g * gw:(g + 1) * gw]).astype(y_ref.dtype)

    @pl.when(c == nc - 1)
    def _():
        ssm_out_ref[0] = s_ref[...]
        tail_ref[0] = tail


def _ssd(y_in, cw, cb, dtb, alog, dskip, gn, conv0, ssm0, layer, *, nb, q, nc, row0):
    rb = row0 // q
    half = SSD_HEADS // 2
    in_specs = [
        pl.BlockSpec((q, CONV_DIM), lambda b, c: (rb + b * nc + c, COL_XBC // CONV_DIM)),
        pl.BlockSpec((q, D_INNER), lambda b, c: (rb + b * nc + c, COL_Z // D_INNER)),
        pl.BlockSpec((q, LANES), lambda b, c: (rb + b * nc + c, COL_DT // LANES)),
        pl.BlockSpec((CONV_W, CONV_DIM), lambda b, c: (0, 0)),
        pl.BlockSpec((1, CONV_DIM), lambda b, c: (0, 0)),
        pl.BlockSpec((1, LANES), lambda b, c: (0, 0)),
        pl.BlockSpec((1, LANES), lambda b, c: (0, 0)),
        pl.BlockSpec((1, D_INNER), lambda b, c: (0, 0)),
        pl.BlockSpec((1, D_INNER), lambda b, c: (0, 0)),
    ]
    args = [y_in, y_in, y_in, cw, cb, dtb, alog, dskip, gn]
    has_init = conv0 is not None
    if has_init:
        in_specs += [
            pl.BlockSpec((None, 1, SUBLANES, CONV_DIM), lambda b, c: (layer, b, 0, 0)),
            pl.BlockSpec((None, 1, half, LANES, D_STATE), lambda b, c: (layer, b, 0, 0, 0)),
        ]
        args += [conv0, ssm0]
    return pl.pallas_call(
        functools.partial(_ssd_kernel, q=q, nc=nc, has_init=has_init),
        grid=(nb, nc),
        in_specs=in_specs,
        out_specs=[
            pl.BlockSpec((q, D_INNER), lambda b, c: (b * nc + c, 0)),
            pl.BlockSpec((1, half, LANES, D_STATE), lambda b, c: (b, 0, 0, 0)),
            pl.BlockSpec((1, SUBLANES, CONV_DIM), lambda b, c: (b, 0, 0)),
        ],
        out_shape=[
            _mixer_out_shape(has_init, nb * nc * q),
            jax.ShapeDtypeStruct((nb, half, LANES, D_STATE), F32),
            jax.ShapeDtypeStruct((nb, SUBLANES, CONV_DIM), F32),
        ],
        scratch_shapes=[
            pltpu.VMEM((q + SUBLANES, CONV_DIM), F32),
            pltpu.VMEM((half, LANES, D_STATE), F32),
        ],
        compiler_params=_cparams(("parallel", "arbitrary")),
        name="ssd_q%d" % q,
    )(*args)


def _half_rmsnorm(x, gain, lo):
    sq = x * x
    ms_lo = jnp.sum(jnp.where(lo, sq, 0.0), axis=-1, keepdims=True) * (1.0 / SWA_HEAD_DIM)
    ms_hi = jnp.sum(jnp.where(lo, 0.0, sq), axis=-1, keepdims=True) * (1.0 / SWA_HEAD_DIM)
    return x * jnp.where(lo, lax.rsqrt(ms_lo + EPS), lax.rsqrt(ms_hi + EPS)) * gain


def _both_halves(x, lo, take_lo):
    r = pltpu.roll(x, SWA_HEAD_DIM, axis=1)
    return jnp.where(lo, x, r) if take_lo else jnp.where(lo, r, x)


def _pad_rows(x, rows):
    if x.shape[0] == rows:
        return x
    return jnp.concatenate([x, jnp.zeros((rows - x.shape[0], x.shape[1]), x.dtype)], axis=0)


def _swa_kernel(sink_ref, slope_ref, q_ref, kc_ref, vc_ref, kp_ref, vp_ref, gq_ref, gk_ref, o_ref, kn_ref,
                *, lq, cached):
    if cached:
        kp, vp = kp_ref[0], vp_ref[0]
    else:
        kp, vp = kp_ref[...], vp_ref[...]
    gq_heads = SWA_HEADS // SWA_KV_HEADS
    rows = gq_heads * lq
    lo_q = lax.broadcasted_iota(jnp.int32, (lq, LANES), 1) < SWA_HEAD_DIM
    lo_w = lax.broadcasted_iota(jnp.int32, (WINDOW, LANES), 1) < SWA_HEAD_DIM
    gq = gq_ref[...]
    gk = gk_ref[...]

    kc = kc_ref[...]
    vc = vc_ref[...]
    kn_slabs, kp_slabs = [], []
    for s in range(2):
        sl = slice(s * LANES, (s + 1) * LANES)
        kn_s = _half_rmsnorm(kc[:, sl], gk, lo_q)
        kn_ref[:, sl] = kn_s
        kn_slabs.append(kn_s)
        kp_slabs.append(kp[:, sl] if cached else _half_rmsnorm(kp[:, sl], gk, lo_w))

    r_idx = lax.broadcasted_iota(jnp.int32, (rows, WINDOW), 0)
    j_idx = lax.broadcasted_iota(jnp.int32, (rows, WINDOW), 1)
    t_idx = r_idx % lq
    mask_prev = j_idx >= t_idx
    if not cached:
        mask_prev = jnp.logical_and(mask_prev, pl.program_id(1) > 0)
    mask_cur = j_idx <= t_idx
    dist_prev = (t_idx + WINDOW - j_idx).astype(F32)
    dist_cur = (t_idx - j_idx).astype(F32)
    head_of_row = lax.broadcasted_iota(jnp.int32, (rows, 1), 0) // lq
    scale = SWA_HEAD_DIM ** -0.5

    for kvh in range(SWA_KV_HEADS):
        slab, take_lo = kvh // 2, (kvh % 2 == 0)
        sl = slice(slab * LANES, (slab + 1) * LANES)
        k_cur = _pad_rows(_both_halves(kn_slabs[slab], lo_q, take_lo), WINDOW).astype(BF16)
        v_cur = _pad_rows(_both_halves(vc[:, sl], lo_q, take_lo), WINDOW).astype(BF16)
        k_prev = _both_halves(kp_slabs[slab], lo_w, take_lo).astype(BF16)
        v_prev = _both_halves(vp[:, sl], lo_w, take_lo).astype(BF16)

        q_parts = []
        for p in range(gq_heads // 2):
            c0 = kvh * gq_heads * SWA_HEAD_DIM + p * LANES
            qn = _half_rmsnorm(q_ref[:, c0:c0 + LANES], gq, lo_q)
            q_parts.append(jnp.where(lo_q, qn, 0.0))
            q_parts.append(jnp.where(lo_q, 0.0, qn))
        qs = jnp.concatenate(q_parts, axis=0).astype(BF16)

        sink = jnp.zeros((rows, 1), F32)
        slope = jnp.zeros((rows, 1), F32)
        for gi in range(gq_heads):
            sink = jnp.where(head_of_row == gi, sink_ref[kvh * gq_heads + gi], sink)
            slope = jnp.where(head_of_row == gi, slope_ref[kvh * gq_heads + gi], slope)

        s_prev = _nt_dot(qs, k_prev) * scale - slope * dist_prev
        s_cur = _nt_dot(qs, k_cur) * scale - slope * dist_cur
        s_prev = jnp.where(mask_prev, s_prev, NEG_INF)
        s_cur = jnp.where(mask_cur, s_cur, NEG_INF)
        m = jnp.maximum(jnp.maximum(jnp.max(s_prev, axis=-1, keepdims=True),
                                    jnp.max(s_cur, axis=-1, keepdims=True)), sink)
        p_prev = jnp.exp(s_prev - m)
        p_cur = jnp.exp(s_cur - m)
        den = (jnp.sum(p_prev, axis=-1, keepdims=True) + jnp.sum(p_cur, axis=-1, keepdims=True)
               + jnp.exp(sink - m))
        o = (jnp.dot(p_prev.astype(BF16), v_prev, preferred_element_type=F32)
             + jnp.dot(p_cur.astype(BF16), v_cur, preferred_element_type=F32)) / den
        for p in range(gq_heads // 2):
            c0 = kvh * gq_heads * SWA_HEAD_DIM + p * LANES
            o_lo = o[(2 * p) * lq:(2 * p + 1) * lq]
            o_hi = o[(2 * p + 1) * lq:(2 * p + 2) * lq]
            o_ref[:, c0:c0 + LANES] = jnp.where(lo_q, o_lo, o_hi).astype(o_ref.dtype)


def _swa(y_in, sinks, slopes, gq2, gk2, win_k, win_v, layer, *, nb, lq, nq, row0):
    rb = row0 // lq
    kw = SWA_KV_HEADS * SWA_HEAD_DIM
    cached = win_k is not None
    in_specs = [
        pl.BlockSpec(memory_space=pltpu.SMEM),
        pl.BlockSpec(memory_space=pltpu.SMEM),
        pl.BlockSpec((lq, D_MODEL), lambda b, i: (rb + b * nq + i, COL_Q // D_MODEL)),
        pl.BlockSpec((lq, kw), lambda b, i: (rb + b * nq + i, COL_K // kw)),
        pl.BlockSpec((lq, kw), lambda b, i: (rb + b * nq + i, COL_V // kw)),
    ]
    args = [sinks, slopes, y_in, y_in, y_in]
    if cached:
        in_specs += [
            pl.BlockSpec((None, 1, WINDOW, kw), lambda b, i: (layer, b, 0, 0)),
            pl.BlockSpec((None, 1, WINDOW, kw), lambda b, i: (layer, b, 0, 0)),
        ]
        args += [win_k, win_v]
    else:
        in_specs += [
            pl.BlockSpec((lq, kw), lambda b, i: (rb + b * nq + jnp.maximum(i - 1, 0), COL_K // kw)),
            pl.BlockSpec((lq, kw), lambda b, i: (rb + b * nq + jnp.maximum(i - 1, 0), COL_V // kw)),
        ]
        args += [y_in, y_in]
    in_specs += [pl.BlockSpec((1, LANES), lambda b, i: (0, 0)), pl.BlockSpec((1, LANES), lambda b, i: (0, 0))]
    args += [gq2, gk2]
    return pl.pallas_call(
        functools.partial(_swa_kernel, lq=lq, cached=cached),
        grid=(nb, nq),
        in_specs=in_specs,
        out_specs=[
            pl.BlockSpec((lq, D_MODEL), lambda b, i: (b * nq + i, 0)),
            pl.BlockSpec((lq, kw), lambda b, i: (b * nq + i, 0)),
        ],
        out_shape=[
            _mixer_out_shape(cached, nb * nq * lq),
            jax.ShapeDtypeStruct((nb * nq * lq, kw), F32),
        ],
        compiler_params=_cparams(("parallel", "arbitrary")),
        name="swa_lq%d" % lq,
    )(*args)


def _mem_attn_kernel(q_ref, k_ref, v_ref, o_ref, *, cached):
    scale = MEM_HEAD_DIM ** -0.5
    for h in range(MEM_HEADS):
        sl = slice(h * MEM_HEAD_DIM, (h + 1) * MEM_HEAD_DIM)
        qh = q_ref[:, sl].astype(BF16)
        if cached:
            kh = k_ref[0, :, sl].astype(BF16)
            vh = v_ref[0, :, sl].astype(BF16)
        else:
            kh = k_ref[:, sl].astype(BF16)
            vh = v_ref[:, sl].astype(BF16)
        s = _nt_dot(qh, kh) * scale
        m = jnp.max(s, axis=-1, keepdims=True)
        p = jnp.exp(s - m)
        den = jnp.sum(p, axis=-1, keepdims=True)
        o_ref[:, sl] = (jnp.dot(p.astype(BF16), vh, preferred_element_type=F32) / den).astype(o_ref.dtype)


def _mem_attn(y_in, mem_k, mem_v, cached, layer, *, nb, tq, nq, row0):
    rb = row0 // tq
    in_specs = [pl.BlockSpec((tq, D_MODEL), lambda b, i: (rb + b * nq + i, COL_QM // D_MODEL))]
    args = [y_in, mem_k, mem_v]
    if cached:
        in_specs += [
            pl.BlockSpec((None, 1, N_MEM, D_MODEL), lambda b, i: (layer, b, 0, 0)),
            pl.BlockSpec((None, 1, N_MEM, D_MODEL), lambda b, i: (layer, b, 0, 0)),
        ]
    else:
        in_specs += [
            pl.BlockSpec((N_MEM, D_MODEL), lambda b, i: (b, 0)),
            pl.BlockSpec((N_MEM, D_MODEL), lambda b, i: (b, 1)),
        ]
    return pl.pallas_call(
        functools.partial(_mem_attn_kernel, cached=cached),
        grid=(nb, nq),
        in_specs=in_specs,
        out_specs=pl.BlockSpec((tq, D_MODEL), lambda b, i: (b * nq + i, 0)),
        out_shape=_mixer_out_shape(cached, nb * nq * tq),
        compiler_params=_cparams(("parallel", "arbitrary")),
        name="mem_attn_tq%d" % tq,
    )(*args)


def _merge_kernel(a0_ref, a1_ref, a2_ref, w0_ref, w1_ref, w2_ref, g0_ref, g1_ref, g2_ref,
                  b0_ref, b1_ref, b2_ref, o_ref):
    acc = None
    for a_ref, w_ref, g_ref, b_ref in ((a0_ref, w0_ref, g0_ref, b0_ref), (a1_ref, w1_ref, g1_ref, b1_ref),
                                       (a2_ref, w2_ref, g2_ref, b2_ref)):
        br = jnp.dot(a_ref[...], w_ref[...], preferred_element_type=F32)
        t = jax.nn.sigmoid(g_ref[...] + b_ref[...]) * br
        acc = t if acc is None else acc + t
    o_ref[...] = acc.astype(BF16)


def _merge(a0, a1, a2, w0, w1, w2, y_in, b_gate, layer, *, tm, tn):
    gb = COL_GATE // tn
    nbk = D_MODEL // tn
    a_spec = pl.BlockSpec((tm, D_MODEL), lambda i, j: (i, 0))
    w_spec = pl.BlockSpec((None, D_MODEL, tn), lambda i, j: (layer, 0, j))
    return pl.pallas_call(
        _merge_kernel,
        grid=(T_ALL // tm, nbk),
        in_specs=[a_spec, a_spec, a_spec, w_spec, w_spec, w_spec,
                  pl.BlockSpec((tm, tn), lambda i, j: (i, gb + j)),
                  pl.BlockSpec((tm, tn), lambda i, j: (i, gb + nbk + j)),
                  pl.BlockSpec((tm, tn), lambda i, j: (i, gb + 2 * nbk + j)),
                  pl.BlockSpec((1, tn), lambda i, j: (0, j)),
                  pl.BlockSpec((1, tn), lambda i, j: (0, nbk + j)),
                  pl.BlockSpec((1, tn), lambda i, j: (0, 2 * nbk + j))],
        out_specs=pl.BlockSpec((tm, tn), lambda i, j: (i, j)),
        out_shape=jax.ShapeDtypeStruct((T_ALL, D_MODEL), BF16),
        compiler_params=_cparams(("parallel", "arbitrary")),
        name="gated_merge",
    )(a0, a1, a2, w0, w1, w2, y_in, y_in, y_in, b_gate, b_gate, b_gate)


def _matmul_res_kernel(a_ref, w_ref, r_ref, o_ref):
    o_ref[...] = r_ref[...] + jnp.dot(a_ref[...], w_ref[...], preferred_element_type=F32)


def _matmul_res(a, w, r, layer, *, tn):
    m, k = a.shape
    n = w.shape[2]
    return pl.pallas_call(
        _matmul_res_kernel,
        grid=(m // TM, n // tn),
        in_specs=[pl.BlockSpec((TM, k), lambda i, j: (i, 0)),
                  pl.BlockSpec((None, k, tn), lambda i, j: (layer, 0, j)),
                  pl.BlockSpec((TM, tn), lambda i, j: (i, j))],
        out_specs=pl.BlockSpec((TM, tn), lambda i, j: (i, j)),
        out_shape=jax.ShapeDtypeStruct((m, n), F32),
        compiler_params=_cparams(("parallel", "arbitrary")),
        name="matmul_residual",
    )(a, w, r)


def _top_values(s, k):
    vals = []
    cur = s
    for r in range(k):
        mx = jnp.max(cur, axis=0, keepdims=True)
        vals.append(mx)
        if r + 1 < k:
            cur = jnp.where(cur == mx, NEG_INF, cur)
    return vals


def _router_kernel(x_ref, g_ref, wqt_ref, keys_ref, h2_ref, e0_ref, th_ref, e1_ref):
    h2 = _rms(x_ref[...], g_ref[...]).astype(BF16)
    h2_ref[...] = h2
    qt = _nt_dot(wqt_ref[...], h2)
    for h in range(PEER_HEADS):
        s0 = jnp.dot(keys_ref[2 * h], qt[(2 * h) * PEER_HALF:(2 * h + 1) * PEER_HALF, :].astype(BF16),
                     preferred_element_type=F32)
        s1 = jnp.dot(keys_ref[2 * h + 1], qt[(2 * h + 1) * PEER_HALF:(2 * h + 2) * PEER_HALF, :].astype(BF16),
                     preferred_element_type=F32)
        top0 = _top_values(s0, PEER_TOPK)
        top1 = _top_values(s1, PEER_TOPK)
        a1 = jnp.concatenate(top1, axis=0)
        half_k = PEER_TOPK // 2
        cand = jnp.concatenate(
            [top0[0] + a1]
            + [top0[r] + a1[:half_k] for r in range(1, half_k)]
            + [jnp.concatenate(top0[half_k:], axis=0) + top1[0]], axis=0)
        best = _top_values(cand, PEER_TOPK)
        m = best[0]
        tau = best[PEER_TOPK - 1]
        z = jnp.zeros_like(m)
        for r in range(PEER_TOPK):
            z = z + jnp.exp(best[r] - m)
        e1_sorted = jnp.exp(a1 - top1[0])
        th = jnp.full(s0.shape, NEVER, F32)
        for r in range(PEER_TOPK):
            th_r = jnp.min(jnp.where((top0[r] + a1) >= tau, e1_sorted, NEVER), axis=0, keepdims=True)
            th = jnp.where(s0 == top0[r], th_r, th)
        e0_ref[h] = 0.5 * jnp.exp(s0 - top0[0]) / z
        th_ref[h] = th
        e1_ref[h] = jnp.exp(s1 - top1[0])


def _router(x, g, wqt, keys, layer, *, tmr):
    t = x.shape[0]
    tok = pl.BlockSpec((PEER_HEADS, N_KEYS, tmr), lambda i: (0, 0, i))
    per_key = jax.ShapeDtypeStruct((PEER_HEADS, N_KEYS, t), F32)
    return pl.pallas_call(
        _router_kernel,
        grid=(t // tmr,),
        in_specs=[pl.BlockSpec((tmr, D_MODEL), lambda i: (i, 0)),
                  pl.BlockSpec((1, D_MODEL), lambda i: (0, 0)),
                  pl.BlockSpec((None, PEER_HEADS * PEER_QUERY_DIM, D_MODEL), lambda i: (layer, 0, 0)),
                  pl.BlockSpec((None, 2 * PEER_HEADS, N_KEYS, PEER_HALF), lambda i: (layer, 0, 0, 0))],
        out_specs=[pl.BlockSpec((tmr, D_MODEL), lambda i: (i, 0)), tok, tok, tok],
        out_shape=[jax.ShapeDtypeStruct((t, D_MODEL), BF16), per_key, per_key, per_key],
        compiler_params=_cparams(("parallel",)),
        name="peer_router",
    )(x, g, wqt, keys)


def _peer_dense_kernel(h2_ref, u_ref, vt_ref, e0_ref, th_ref, e1_ref, x_ref, o_ref,
                       acc_ref, at_ref, wa_ref, wb_ref, h2t_ref, *, te, ne, tm):
    e = pl.program_id(1)
    per_step = te // N_KEYS
    assert 2 * per_step == SUBLANES

    @pl.when(e == 0)
    def _():
        acc_ref[...] = jnp.zeros_like(acc_ref)
        wb_ref[...] = jnp.zeros_like(wb_ref)
        h2t_ref[...] = h2_ref[...].T

    chunk = 2 * LANES
    n_chunks = tm // chunk
    vrows = D_MODEL // (2 * per_step)

    def value_piece(w_old_ref, c, m):
        rows = slice(m * vrows, (m + 1) * vrows)
        cols = slice(c * chunk, (c + 1) * chunk)
        acc_ref[rows, cols] += jnp.dot(vt_ref[rows, :], w_old_ref[:, cols], preferred_element_type=F32)

    def step(w_new_ref, w_old_ref, parity):
        def score_piece(c, ii):
            rows = slice(ii * N_KEYS, (ii + 1) * N_KEYS)
            cols = slice(c * chunk, (c + 1) * chunk)
            at_ref[rows, cols] = jnp.dot(u_ref[rows, :], h2t_ref[:, cols], preferred_element_type=F32)

        def gate_cell(ii, tg):
            rows = slice(ii * N_KEYS, (ii + 1) * N_KEYS)
            cs = slice(tg * LANES, (tg + 1) * LANES)
            r = parity * per_step + ii
            gate = None
            for h in range(PEER_HEADS):
                e1 = e1_ref[h, :, cs]
                t = jnp.where(e1 >= th_ref[h, r:r + 1, cs], e1 * e0_ref[h, r:r + 1, cs], 0.0)
                gate = t if gate is None else gate + t
            a = at_ref[rows, cs]
            gelu2 = a * (1.0 + lax.erf(a * (2.0 ** -0.5)))
            w_new_ref[rows, cs] = (gate * gelu2).astype(BF16)

        rounds = [(c, ii) for c in range(n_chunks) for ii in range(per_step)]
        score_piece(*rounds[0])
        for k, (c, ii) in enumerate(rounds):
            if k + 1 < len(rounds):
                score_piece(*rounds[k + 1])
            gate_cell(ii, 2 * c)
            value_piece(w_old_ref, c, 2 * ii)
            gate_cell(ii, 2 * c + 1)
            value_piece(w_old_ref, c, 2 * ii + 1)

    @pl.when(jnp.logical_and(e % 2 == 0, e < ne))
    def _():
        step(wa_ref, wb_ref, 0)

    @pl.when(jnp.logical_and(e % 2 == 1, e < ne))
    def _():
        step(wb_ref, wa_ref, 1)

    @pl.when(e == ne)
    def _():
        w_last_ref = wb_ref if ne % 2 == 0 else wa_ref
        for c in range(n_chunks):
            for m in range(D_MODEL // vrows):
                value_piece(w_last_ref, c, m)
        o_ref[...] = x_ref[...] + acc_ref[...].T


def _peer_dense(h2, u, vt, e0, th, e1, x, layer, *, te, tm):
    t = x.shape[0]
    ne = N_EXPERTS // te
    once = pl.Buffered(1)
    by_first = pl.BlockSpec((PEER_HEADS, SUBLANES, tm), lambda i, e: (0, jnp.minimum(e, ne - 1) // 2, i))
    by_second = pl.BlockSpec((PEER_HEADS, N_KEYS, tm), lambda i, e: (0, 0, i), pipeline_mode=once)
    return pl.pallas_call(
        functools.partial(_peer_dense_kernel, te=te, ne=ne, tm=tm),
        grid=(t // tm, ne + 1),
        in_specs=[pl.BlockSpec((tm, D_MODEL), lambda i, e: (i, 0), pipeline_mode=once),
                  pl.BlockSpec((None, te, D_MODEL), lambda i, e: (layer, jnp.minimum(e, ne - 1), 0)),
                  pl.BlockSpec((None, D_MODEL, te), lambda i, e: (layer, 0, jnp.maximum(e - 1, 0))),
                  by_first, by_first, by_second,
                  pl.BlockSpec((tm, D_MODEL), lambda i, e: (i, 0), pipeline_mode=once)],
        out_specs=pl.BlockSpec((tm, D_MODEL), lambda i, e: (i, 0)),
        out_shape=jax.ShapeDtypeStruct((t, D_MODEL), F32),
        scratch_shapes=[pltpu.VMEM((D_MODEL, tm), F32), pltpu.VMEM((te, tm), F32),
                        pltpu.VMEM((te, tm), BF16), pltpu.VMEM((te, tm), BF16),
                        pltpu.VMEM((D_MODEL, tm), BF16)],
        compiler_params=_cparams(("parallel", "arbitrary")),
        name="peer_dense",
    )(h2, u, vt, e0, th, e1, x)


def _join_groups(prompt_buf, sample_rows):
    return lax.dynamic_update_slice(prompt_buf, sample_rows.astype(BF16), (T_PROMPT, 0))


def _pad_lanes(v, n=LANES):
    return jnp.pad(v, (0, n - v.shape[0])).reshape(1, n)


def kernel(x_prompt, x_sample, cache_win_k, cache_win_v, state_ssm, state_conv, cache_mem_k, cache_mem_v,
           mem_prompt, w_in, b_gate, conv_w, conv_b, dt_bias, a_log, d_skip, g_ssd_norm, g_q, g_k,
           attn_sinks, g_mem, w_mem_kv, g_qm, g_km, w_o_ssm, w_o_swa, w_o_mem, w_out, g_mix, g_ffn,
           w_peer_q, peer_sub_keys, peer_u, peer_v):
    x = jnp.concatenate([x_prompt.reshape(T_PROMPT, D_MODEL), x_sample.reshape(T_SAMPLE, D_MODEL)], axis=0)
    slopes = jnp.exp2(-8.0 * jnp.arange(1, SWA_HEADS + 1, dtype=F32) / SWA_HEADS)
    mem_x = mem_prompt.reshape(BATCH * N_MEM, D_MODEL)
    kw = SWA_KV_HEADS * SWA_HEAD_DIM
    half = SSD_HEADS // 2
    row = lambda v: v.reshape(1, -1)

    w_in_t = jnp.swapaxes(w_in, 1, 2)
    w_mem_kv_b = w_mem_kv.astype(BF16)
    w_o_ssm_b, w_o_swa_b, w_o_mem_b = w_o_ssm.astype(BF16), w_o_swa.astype(BF16), w_o_mem.astype(BF16)
    w_out_b = w_out.astype(BF16)
    wqt_b = jnp.swapaxes(w_peer_q, 1, 2).astype(BF16)
    keys_b = peer_sub_keys.reshape(DEPTH, 2 * PEER_HEADS, N_KEYS, PEER_HALF).astype(BF16)
    u_b = peer_u.astype(BF16)
    vt_b = jnp.swapaxes(peer_v, 1, 2).astype(BF16)
    conv0 = jnp.pad(state_conv, ((0, 0), (0, 0), (SUBLANES - (CONV_W - 1), 0), (0, 0)))
    ssm0 = state_ssm.reshape(DEPTH, DEC_BATCH, half, LANES, D_STATE)
    wk_c = cache_win_k.reshape(DEPTH, DEC_BATCH, WINDOW, kw)
    wv_c = cache_win_v.reshape(DEPTH, DEC_BATCH, WINDOW, kw)
    mk_c = cache_mem_k.reshape(DEPTH, DEC_BATCH, N_MEM, D_MODEL)
    mv_c = cache_mem_v.reshape(DEPTH, DEC_BATCH, N_MEM, D_MODEL)

    outs = {k: [] for k in ("p_wk", "p_wv", "p_ssm", "p_conv", "p_mk", "p_mv", "s_wk", "s_wv", "s_ssm", "s_conv")}
    for l in range(DEPTH):
        mem_kv = _norm_matmul(mem_x, row(g_mem[l]), w_mem_kv_b, row(g_km[l]), l,
                              tm=512, tn=MEM_HEAD_DIM, norm_lo=0, norm_hi=MEM_HEADS)
        outs["p_mk"].append(mem_kv[:, :D_MODEL].reshape(BATCH, N_MEM, MEM_HEADS, MEM_HEAD_DIM))
        outs["p_mv"].append(mem_kv[:, D_MODEL:].reshape(BATCH, N_MEM, MEM_HEADS, MEM_HEAD_DIM))

        y = _in_proj(_rms_cast(x, row(g_mix[l])), w_in_t, row(g_qm[l]), l)

        cw, cb = conv_w[l], row(conv_b[l])
        dtb, alog = _pad_lanes(dt_bias[l]), _pad_lanes(a_log[l])
        dskip = row(jnp.repeat(d_skip[l], SSD_HEAD_DIM))
        gn = row(g_ssd_norm[l])
        y_ssm, p_ssm, p_tail = _ssd(y, cw, cb, dtb, alog, dskip, gn, None, None, l,
                                    nb=BATCH, q=SSD_CHUNK, nc=SEQ // SSD_CHUNK, row0=0)
        y_ssm_s, s_ssm, s_tail = _ssd(y, cw, cb, dtb, alog, dskip, gn, conv0, ssm0, l,
                                      nb=DEC_BATCH, q=DEC_SEQ, nc=1, row0=T_PROMPT)
        y_ssm = _join_groups(y_ssm, y_ssm_s)
        outs["p_ssm"].append(p_ssm.reshape(BATCH, SSD_HEADS, SSD_HEAD_DIM, D_STATE))
        outs["s_ssm"].append(s_ssm.reshape(DEC_BATCH, SSD_HEADS, SSD_HEAD_DIM, D_STATE))
        outs["p_conv"].append(p_tail[:, SUBLANES - (CONV_W - 1):])
        outs["s_conv"].append(s_tail[:, SUBLANES - (CONV_W - 1):])

        gq2 = row(jnp.concatenate([g_q[l], g_q[l]]))
        gk2 = row(jnp.concatenate([g_k[l], g_k[l]]))
        o_swa, kn_p = _swa(y, attn_sinks[l], slopes, gq2, gk2, None, None, l,
                           nb=BATCH, lq=WINDOW, nq=SEQ // WINDOW, row0=0)
        o_swa_s, kn_s = _swa(y, attn_sinks[l], slopes, gq2, gk2, wk_c, wv_c, l,
                             nb=DEC_BATCH, lq=DEC_SEQ, nq=1, row0=T_PROMPT)
        o_swa = _join_groups(o_swa, o_swa_s)
        v_p = y[:T_PROMPT, COL_V:COL_V + kw].reshape(BATCH, SEQ, kw)
        v_s = y[T_PROMPT:, COL_V:COL_V + kw].reshape(DEC_BATCH, DEC_SEQ, kw)
        kv_shape = (SWA_KV_HEADS, SWA_HEAD_DIM)
        outs["p_wk"].append(kn_p.reshape(BATCH, SEQ, kw)[:, SEQ - WINDOW:].reshape(BATCH, WINDOW, *kv_shape))
        outs["p_wv"].append(v_p[:, SEQ - WINDOW:].reshape(BATCH, WINDOW, *kv_shape))
        outs["s_wk"].append(jnp.concatenate([wk_c[l][:, DEC_SEQ:], kn_s.reshape(DEC_BATCH, DEC_SEQ, kw)], axis=1)
                            .reshape(DEC_BATCH, WINDOW, *kv_shape))
        outs["s_wv"].append(jnp.concatenate([wv_c[l][:, DEC_SEQ:], v_s], axis=1)
                            .reshape(DEC_BATCH, WINDOW, *kv_shape))

        o_mem = _mem_attn(y, mem_kv, mem_kv, False, l, nb=BATCH, tq=512, nq=SEQ // 512, row0=0)
        o_mem_s = _mem_attn(y, mk_c, mv_c, True, l, nb=DEC_BATCH, tq=DEC_SEQ, nq=1, row0=T_PROMPT)
        o_mem = _join_groups(o_mem, o_mem_s)

        merged = _merge(y_ssm, o_swa, o_mem, w_o_ssm_b, w_o_swa_b, w_o_mem_b, y, row(b_gate[l]), l,
                        tm=TM, tn=512)
        x = _matmul_res(merged, w_out_b, x, l, tn=512)

        h2, e0, th, e1 = _router(x, row(g_ffn[l]), wqt_b, keys_b, l, tmr=256)
        x = _peer_dense(h2, u_b, vt_b, e0, th, e1, x, l, te=512, tm=TM)

    yp = x[:T_PROMPT].reshape(BATCH, SEQ, D_MODEL)
    ys = x[T_PROMPT:].reshape(DEC_BATCH, DEC_SEQ, D_MODEL)
    st = lambda k: jnp.stack(outs[k])
    return (yp, ys, st("p_wk"), st("p_wv"), st("p_ssm"), st("p_conv"), st("p_mk"), st("p_mv"),
            st("s_wk"), st("s_wv"), st("s_ssm"), st("s_conv"))
```

```python
import functools

import jax
import jax.numpy as jnp
from jax import lax
from jax.experimental import pallas as pl
from jax.experimental.pallas import tpu as pltpu

F32 = jnp.float32
BF16 = jnp.bfloat16

D_MODEL = 2048
BATCH = 4
SEQ = 2048
DEPTH = 2
DEC_BATCH = 32
DEC_SEQ = 8
D_INNER = D_MODEL
SSD_HEAD_DIM = 64
SSD_HEADS = D_INNER // SSD_HEAD_DIM
SSD_GROUPS = 4
D_STATE = 128
CONV_W = 4
CONV_DIM = D_INNER + 2 * SSD_GROUPS * D_STATE
SSD_CHUNK = 128
SWA_HEAD_DIM = 64
SWA_HEADS = D_MODEL // SWA_HEAD_DIM
SWA_KV_HEADS = SWA_HEADS // 8
WINDOW = 128
N_MEM = 256
MEM_HEADS = 4
MEM_HEAD_DIM = D_MODEL // MEM_HEADS
PEER_HEADS = 8
N_KEYS = 128
N_EXPERTS = N_KEYS * N_KEYS
PEER_QUERY_DIM = 256
PEER_HALF = PEER_QUERY_DIM // 2
PEER_TOPK = 16
N_BRANCH = 3
EPS = 1e-6

LANES = 128
SUBLANES = 8
VMEM_LIMIT = 56 * 1024 * 1024

T_PROMPT = BATCH * SEQ
T_SAMPLE = DEC_BATCH * DEC_SEQ
T_ALL = T_PROMPT + T_SAMPLE
TM = 768

COL_XBC = 0
COL_K = CONV_DIM
COL_V = COL_K + 256
COL_DT = COL_V + 256
COL_Z = 4096
COL_Q = COL_Z + D_INNER
COL_QM = COL_Q + D_MODEL
COL_GATE = COL_QM + D_MODEL
IN_PAD = COL_GATE + N_BRANCH * D_MODEL

NEG_INF = float("-inf")
NEVER = 2.0


def _cparams(sem, flags=None):
    return pltpu.CompilerParams(dimension_semantics=sem, vmem_limit_bytes=VMEM_LIMIT, flags=flags)


def _nt_dot(a, b):
    return lax.dot_general(a, b, (((1,), (1,)), ((), ())), preferred_element_type=F32)


def _tn_dot(a, b):
    return lax.dot_general(a, b, (((0,), (0,)), ((), ())), preferred_element_type=F32)


def _mixer_out_shape(carried, rows):
    if carried:
        return jax.ShapeDtypeStruct((rows, D_MODEL), F32)
    return jax.ShapeDtypeStruct((T_ALL, D_MODEL), BF16)


def _rms(y, gain):
    ms = jnp.mean(y * y, axis=-1, keepdims=True)
    return y * lax.rsqrt(ms + EPS) * gain


def _norm_matmul_kernel(x_ref, g_ref, w_ref, gn_ref, o_ref, h_ref, *, norm_lo, norm_hi):
    n = pl.program_id(1)

    @pl.when(n == 0)
    def _():
        h_ref[...] = _rms(x_ref[...], g_ref[...]).astype(BF16)

    y = jnp.dot(h_ref[...], w_ref[...], preferred_element_type=F32)
    is_norm = jnp.logical_and(n >= norm_lo, n < norm_hi)

    @pl.when(is_norm)
    def _():
        for c in range(y.shape[1] // MEM_HEAD_DIM):
            sl = slice(c * MEM_HEAD_DIM, (c + 1) * MEM_HEAD_DIM)
            o_ref[:, sl] = _rms(y[:, sl], gn_ref[...])

    @pl.when(jnp.logical_not(is_norm))
    def _():
        o_ref[...] = y


def _norm_matmul(x, g, w, gn, layer, *, tm, tn, norm_lo, norm_hi):
    m, k = x.shape
    n = w.shape[2]
    return pl.pallas_call(
        functools.partial(_norm_matmul_kernel, norm_lo=norm_lo, norm_hi=norm_hi),
        grid=(m // tm, n // tn),
        in_specs=[
            pl.BlockSpec((tm, k), lambda i, j: (i, 0)),
            pl.BlockSpec((1, k), lambda i, j: (0, 0)),
            pl.BlockSpec((None, k, tn), lambda i, j: (layer, 0, j)),
            pl.BlockSpec((1, MEM_HEAD_DIM), lambda i, j: (0, 0)),
        ],
        out_specs=pl.BlockSpec((tm, tn), lambda i, j: (i, j)),
        out_shape=jax.ShapeDtypeStruct((m, n), F32),
        scratch_shapes=[pltpu.VMEM((tm, k), BF16)],
        compiler_params=_cparams(("parallel", "arbitrary")),
        name="norm_matmul",
    )(x, g, w, gn)


IN_UNIT = 256
IN_TILE = 1024


def _w_in_row_map():
    z0, xbc0 = 0, D_INNER
    dt0 = xbc0 + CONV_DIM
    q0 = dt0 + SSD_HEADS
    k0 = q0 + D_MODEL
    v0 = k0 + 256
    qm0 = v0 + 256
    gate0 = qm0 + D_MODEL
    rows = {}
    for dst, src, width in ((COL_XBC, xbc0, CONV_DIM), (COL_K, k0, 256), (COL_V, v0, 256),
                            (COL_DT, dt0, IN_UNIT), (COL_DT + IN_UNIT, dt0, IN_UNIT), (COL_Z, z0, D_INNER),
                            (COL_Q, q0, D_MODEL), (COL_QM, qm0, D_MODEL), (COL_GATE, gate0, N_BRANCH * D_MODEL)):
        for u in range(width // IN_UNIT):
            rows[dst // IN_UNIT + u] = src + u * IN_UNIT
    return [rows[d] for d in range(IN_PAD // IN_UNIT)]


def _rms_cast_kernel(x_ref, g_ref, o_ref):
    o_ref[...] = _rms(x_ref[...], g_ref[...]).astype(BF16)


def _rms_cast(x, g):
    m, k = x.shape
    return pl.pallas_call(
        _rms_cast_kernel,
        grid=(m // TM,),
        in_specs=[pl.BlockSpec((TM, k), lambda i: (i, 0)), pl.BlockSpec((1, k), lambda i: (0, 0))],
        out_specs=pl.BlockSpec((TM, k), lambda i: (i, 0)),
        out_shape=jax.ShapeDtypeStruct((m, k), BF16),
        compiler_params=_cparams(("parallel",)),
        name="rms_cast",
    )(x, g)


def _in_proj_kernel(rows_ref, h_ref, w0_ref, w1_ref, w2_ref, w3_ref, gn_ref, o_ref, wb_ref, *, norm_lo, norm_hi):
    del rows_ref
    n = pl.program_id(0)

    @pl.when(pl.program_id(1) == 0)
    def _():
        for u, w_ref in enumerate((w0_ref, w1_ref, w2_ref, w3_ref)):
            wb_ref[u * IN_UNIT:(u + 1) * IN_UNIT, :] = w_ref[...].astype(BF16)

    y = _nt_dot(h_ref[...], wb_ref[...])
    is_norm = jnp.logical_and(n >= norm_lo, n < norm_hi)

    @pl.when(is_norm)
    def _():
        for c in range(IN_TILE // MEM_HEAD_DIM):
            sl = slice(c * MEM_HEAD_DIM, (c + 1) * MEM_HEAD_DIM)
            o_ref[:, sl] = _rms(y[:, sl], gn_ref[...])

    @pl.when(jnp.logical_not(is_norm))
    def _():
        o_ref[...] = y


def _in_proj(h, w_t, gn, layer):
    t = h.shape[0]
    per_tile = IN_TILE // IN_UNIT

    def w_spec(u):
        return pl.BlockSpec((None, pl.Element(IN_UNIT), pl.Element(D_MODEL)),
                            lambda n, m, rows: (layer, pl.multiple_of(rows[per_tile * n + u], SSD_HEADS), 0))

    grid_spec = pltpu.PrefetchScalarGridSpec(
        num_scalar_prefetch=1,
        grid=(IN_PAD // IN_TILE, t // TM),
        in_specs=[pl.BlockSpec((TM, D_MODEL), lambda n, m, rows: (m, 0))]
        + [w_spec(u) for u in range(per_tile)]
        + [pl.BlockSpec((1, MEM_HEAD_DIM), lambda n, m, rows: (0, 0))],
        out_specs=pl.BlockSpec((TM, IN_TILE), lambda n, m, rows: (m, n)),
        scratch_shapes=[pltpu.VMEM((IN_TILE, D_MODEL), BF16)],
    )
    return pl.pallas_call(
        functools.partial(_in_proj_kernel, norm_lo=COL_QM // IN_TILE, norm_hi=COL_GATE // IN_TILE),
        grid_spec=grid_spec,
        out_shape=jax.ShapeDtypeStruct((t, IN_PAD), F32),
        compiler_params=_cparams(("parallel", "arbitrary")),
        name="in_proj",
    )(jnp.asarray(_w_in_row_map(), jnp.int32), h, w_t, w_t, w_t, w_t, gn)


def _ssd_kernel(*refs, q, nc, has_init):
    if has_init:
        (xbc_ref, z_ref, dt_ref, cw_ref, cb_ref, dtb_ref, alog_ref, dskip_ref, gn_ref, conv0_ref, ssm0_ref,
         y_ref, ssm_out_ref, tail_ref, ext_ref, s_ref) = refs
    else:
        (xbc_ref, z_ref, dt_ref, cw_ref, cb_ref, dtb_ref, alog_ref, dskip_ref, gn_ref,
         y_ref, ssm_out_ref, tail_ref, ext_ref, s_ref) = refs
    c = pl.program_id(1)

    @pl.when(c == 0)
    def _():
        if has_init:
            ext_ref[0:SUBLANES, :] = conv0_ref[0]
            s_ref[...] = ssm0_ref[0]
        else:
            ext_ref[0:SUBLANES, :] = jnp.zeros((SUBLANES, CONV_DIM), F32)
            s_ref[...] = jnp.zeros_like(s_ref)

    ext_ref[SUBLANES:SUBLANES + q, :] = xbc_ref[...]
    w = cw_ref[...]
    conv = (cb_ref[...]
            + w[3:4] * ext_ref[SUBLANES:SUBLANES + q, :]
            + w[2:3] * ext_ref[SUBLANES - 1:SUBLANES - 1 + q, :]
            + w[1:2] * ext_ref[SUBLANES - 2:SUBLANES - 2 + q, :]
            + w[0:1] * ext_ref[SUBLANES - 3:SUBLANES - 3 + q, :])
    tail = ext_ref[q:q + SUBLANES, :]
    ext_ref[0:SUBLANES, :] = tail
    xc = conv * jax.nn.sigmoid(conv)

    gn_w = SSD_GROUPS * D_STATE
    bm = xc[:, D_INNER:D_INNER + gn_w].astype(BF16)
    cm = xc[:, D_INNER + gn_w:].astype(BF16)

    dtr = dt_ref[...] + dtb_ref[...]
    dt = jnp.maximum(dtr, 0.0) + jnp.log1p(jnp.exp(-jnp.abs(dtr)))
    a_neg = -jnp.exp(alog_ref[...])
    da = dt * a_neg
    row = lax.broadcasted_iota(jnp.int32, (q, q), 0)
    col = lax.broadcasted_iota(jnp.int32, (q, q), 1)
    causal = row >= col
    tri = causal.astype(F32)
    cum = jnp.dot(tri, da, preferred_element_type=F32, precision=lax.Precision.HIGHEST)
    if q < LANES:
        cum_sq = jnp.concatenate([cum, jnp.zeros((LANES - q, LANES), F32)], axis=0)
    else:
        cum_sq = cum
    cum_t = cum_sq.T[:, :q]
    ecum = jnp.exp(cum)
    cl = cum[q - 1:q, :]
    wend = jnp.exp(cl - cum)
    ecl = jnp.exp(cl)

    lo = lax.broadcasted_iota(jnp.int32, (q, LANES), 1) < SSD_HEAD_DIM
    row_lo = lax.broadcasted_iota(jnp.int32, (LANES, LANES), 0) < SSD_HEAD_DIM
    pairs_per_group = SSD_HEADS // SSD_GROUPS // 2

    for g in range(SSD_GROUPS):
        bg = bm[:, g * D_STATE:(g + 1) * D_STATE]
        cg = cm[:, g * D_STATE:(g + 1) * D_STATE]
        cb = _nt_dot(cg, bg)
        y_parts = []
        for pp in range(pairs_per_group):
            hp = g * pairs_per_group + pp
            ha, hb = 2 * hp, 2 * hp + 1
            sl = slice(hp * LANES, (hp + 1) * LANES)
            xp = xc[:, sl]
            xdt = xp * jnp.where(lo, dt[:, ha:ha + 1], dt[:, hb:hb + 1])
            xdt_b = xdt.astype(BF16)
            seg_a = cum[:, ha:ha + 1] - cum_t[ha:ha + 1, :]
            seg_b = cum[:, hb:hb + 1] - cum_t[hb:hb + 1, :]
            m_a = (cb * jnp.exp(jnp.where(causal, seg_a, NEG_INF))).astype(BF16)
            m_b = (cb * jnp.exp(jnp.where(causal, seg_b, NEG_INF))).astype(BF16)
            y_a = jnp.dot(m_a, xdt_b, preferred_element_type=F32)
            y_b = jnp.dot(m_b, xdt_b, preferred_element_type=F32)
            y = jnp.where(lo, y_a, y_b)
            sp = s_ref[hp]
            cs = _nt_dot(cg, sp.astype(BF16))
            y = y + cs * jnp.where(lo, ecum[:, ha:ha + 1], ecum[:, hb:hb + 1])
            xw = (xdt * jnp.where(lo, wend[:, ha:ha + 1], wend[:, hb:hb + 1])).astype(BF16)
            upd = _tn_dot(xw, bg)
            dec = jnp.where(row_lo, ecl[:, ha:ha + 1], ecl[:, hb:hb + 1])
            s_ref[hp] = sp * dec + upd
            y = y + dskip_ref[:, sl] * xp
            zz = z_ref[:, sl]
            y = y * (zz * jax.nn.sigmoid(zz))
            y_parts.append(y)
        yg = jnp.concatenate(y_parts, axis=1)
        gw = D_INNER // SSD_GROUPS
        y_ref[:, g * gw:(g + 1) * gw] = _rms(yg, gn_ref[:, g * gw:(g + 1) * gw]).astype(y_ref.dtype)

    @pl.when(c == nc - 1)
    def _():
        ssm_out_ref[0] = s_ref[...]
        tail_ref[0] = tail


def _ssd(y_in, cw, cb, dtb, alog, dskip, gn, conv0, ssm0, layer, *, nb, q, nc, row0):
    rb = row0 // q
    half = SSD_HEADS // 2
    in_specs = [
        pl.BlockSpec((q, CONV_DIM), lambda b, c: (rb + b * nc + c, COL_XBC // CONV_DIM)),
        pl.BlockSpec((q, D_INNER), lambda b, c: (rb + b * nc + c, COL_Z // D_INNER)),
        pl.BlockSpec((q, LANES), lambda b, c: (rb + b * nc + c, COL_DT // LANES)),
        pl.BlockSpec((CONV_W, CONV_DIM), lambda b, c: (0, 0)),
        pl.BlockSpec((1, CONV_DIM), lambda b, c: (0, 0)),
        pl.BlockSpec((1, LANES), lambda b, c: (0, 0)),
        pl.BlockSpec((1, LANES), lambda b, c: (0, 0)),
        pl.BlockSpec((1, D_INNER), lambda b, c: (0, 0)),
        pl.BlockSpec((1, D_INNER), lambda b, c: (0, 0)),
    ]
    args = [y_in, y_in, y_in, cw, cb, dtb, alog, dskip, gn]
    has_init = conv0 is not None
    if has_init:
        in_specs += [
            pl.BlockSpec((None, 1, SUBLANES, CONV_DIM), lambda b, c: (layer, b, 0, 0)),
            pl.BlockSpec((None, 1, half, LANES, D_STATE), lambda b, c: (layer, b, 0, 0, 0)),
        ]
        args += [conv0, ssm0]
    return pl.pallas_call(
        functools.partial(_ssd_kernel, q=q, nc=nc, has_init=has_init),
        grid=(nb, nc),
        in_specs=in_specs,
        out_specs=[
            pl.BlockSpec((q, D_INNER), lambda b, c: (b * nc + c, 0)),
            pl.BlockSpec((1, half, LANES, D_STATE), lambda b, c: (b, 0, 0, 0)),
            pl.BlockSpec((1, SUBLANES, CONV_DIM), lambda b, c: (b, 0, 0)),
        ],
        out_shape=[
            _mixer_out_shape(has_init, nb * nc * q),
            jax.ShapeDtypeStruct((nb, half, LANES, D_STATE), F32),
            jax.ShapeDtypeStruct((nb, SUBLANES, CONV_DIM), F32),
        ],
        scratch_shapes=[
            pltpu.VMEM((q + SUBLANES, CONV_DIM), F32),
            pltpu.VMEM((half, LANES, D_STATE), F32),
        ],
        compiler_params=_cparams(("parallel", "arbitrary")),
        name="ssd_q%d" % q,
    )(*args)


def _half_rmsnorm(x, gain, lo):
    sq = x * x
    ms_lo = jnp.sum(jnp.where(lo, sq, 0.0), axis=-1, keepdims=True) * (1.0 / SWA_HEAD_DIM)
    ms_hi = jnp.sum(jnp.where(lo, 0.0, sq), axis=-1, keepdims=True) * (1.0 / SWA_HEAD_DIM)
    return x * jnp.where(lo, lax.rsqrt(ms_lo + EPS), lax.rsqrt(ms_hi + EPS)) * gain


def _both_halves(x, lo, take_lo):
    r = pltpu.roll(x, SWA_HEAD_DIM, axis=1)
    return jnp.where(lo, x, r) if take_lo else jnp.where(lo, r, x)


def _pad_rows(x, rows):
    if x.shape[0] == rows:
        return x
    return jnp.concatenate([x, jnp.zeros((rows - x.shape[0], x.shape[1]), x.dtype)], axis=0)


def _swa_kernel(sink_ref, slope_ref, q_ref, kc_ref, vc_ref, kp_ref, vp_ref, gq_ref, gk_ref, o_ref, kn_ref,
                *, lq, cached):
    if cached:
        kp, vp = kp_ref[0], vp_ref[0]
    else:
        kp, vp = kp_ref[...], vp_ref[...]
    gq_heads = SWA_HEADS // SWA_KV_HEADS
    rows = gq_heads * lq
    lo_q = lax.broadcasted_iota(jnp.int32, (lq, LANES), 1) < SWA_HEAD_DIM
    lo_w = lax.broadcasted_iota(jnp.int32, (WINDOW, LANES), 1) < SWA_HEAD_DIM
    gq = gq_ref[...]
    gk = gk_ref[...]

    kc = kc_ref[...]
    vc = vc_ref[...]
    kn_slabs, kp_slabs = [], []
    for s in range(2):
        sl = slice(s * LANES, (s + 1) * LANES)
        kn_s = _half_rmsnorm(kc[:, sl], gk, lo_q)
        kn_ref[:, sl] = kn_s
        kn_slabs.append(kn_s)
        kp_slabs.append(kp[:, sl] if cached else _half_rmsnorm(kp[:, sl], gk, lo_w))

    r_idx = lax.broadcasted_iota(jnp.int32, (rows, WINDOW), 0)
    j_idx = lax.broadcasted_iota(jnp.int32, (rows, WINDOW), 1)
    t_idx = r_idx % lq
    mask_prev = j_idx >= t_idx
    if not cached:
        mask_prev = jnp.logical_and(mask_prev, pl.program_id(1) > 0)
    mask_cur = j_idx <= t_idx
    dist_prev = (t_idx + WINDOW - j_idx).astype(F32)
    dist_cur = (t_idx - j_idx).astype(F32)
    head_of_row = lax.broadcasted_iota(jnp.int32, (rows, 1), 0) // lq
    scale = SWA_HEAD_DIM ** -0.5

    for kvh in range(SWA_KV_HEADS):
        slab, take_lo = kvh // 2, (kvh % 2 == 0)
        sl = slice(slab * LANES, (slab + 1) * LANES)
        k_cur = _pad_rows(_both_halves(kn_slabs[slab], lo_q, take_lo), WINDOW).astype(BF16)
        v_cur = _pad_rows(_both_halves(vc[:, sl], lo_q, take_lo), WINDOW).astype(BF16)
        k_prev = _both_halves(kp_slabs[slab], lo_w, take_lo).astype(BF16)
        v_prev = _both_halves(vp[:, sl], lo_w, take_lo).astype(BF16)

        q_parts = []
        for p in range(gq_heads // 2):
            c0 = kvh * gq_heads * SWA_HEAD_DIM + p * LANES
            qn = _half_rmsnorm(q_ref[:, c0:c0 + LANES], gq, lo_q)
            q_parts.append(jnp.where(lo_q, qn, 0.0))
            q_parts.append(jnp.where(lo_q, 0.0, qn))
        qs = jnp.concatenate(q_parts, axis=0).astype(BF16)

        sink = jnp.zeros((rows, 1), F32)
        slope = jnp.zeros((rows, 1), F32)
        for gi in range(gq_heads):
            sink = jnp.where(head_of_row == gi, sink_ref[kvh * gq_heads + gi], sink)
            slope = jnp.where(head_of_row == gi, slope_ref[kvh * gq_heads + gi], slope)

        s_prev = _nt_dot(qs, k_prev) * scale - slope * dist_prev
        s_cur = _nt_dot(qs, k_cur) * scale - slope * dist_cur
        s_prev = jnp.where(mask_prev, s_prev, NEG_INF)
        s_cur = jnp.where(mask_cur, s_cur, NEG_INF)
        m = jnp.maximum(jnp.maximum(jnp.max(s_prev, axis=-1, keepdims=True),
                                    jnp.max(s_cur, axis=-1, keepdims=True)), sink)
        p_prev = jnp.exp(s_prev - m)
        p_cur = jnp.exp(s_cur - m)
        den = (jnp.sum(p_prev, axis=-1, keepdims=True) + jnp.sum(p_cur, axis=-1, keepdims=True)
               + jnp.exp(sink - m))
        o = (jnp.dot(p_prev.astype(BF16), v_prev, preferred_element_type=F32)
             + jnp.dot(p_cur.astype(BF16), v_cur, preferred_element_type=F32)) / den
        for p in range(gq_heads // 2):
            c0 = kvh * gq_heads * SWA_HEAD_DIM + p * LANES
            o_lo = o[(2 * p) * lq:(2 * p + 1) * lq]
            o_hi = o[(2 * p + 1) * lq:(2 * p + 2) * lq]
            o_ref[:, c0:c0 + LANES] = jnp.where(lo_q, o_lo, o_hi).astype(o_ref.dtype)


def _swa(y_in, sinks, slopes, gq2, gk2, win_k, win_v, layer, *, nb, lq, nq, row0):
    rb = row0 // lq
    kw = SWA_KV_HEADS * SWA_HEAD_DIM
    cached = win_k is not None
    in_specs = [
        pl.BlockSpec(memory_space=pltpu.SMEM),
        pl.BlockSpec(memory_space=pltpu.SMEM),
        pl.BlockSpec((lq, D_MODEL), lambda b, i: (rb + b * nq + i, COL_Q // D_MODEL)),
        pl.BlockSpec((lq, kw), lambda b, i: (rb + b * nq + i, COL_K // kw)),
        pl.BlockSpec((lq, kw), lambda b, i: (rb + b * nq + i, COL_V // kw)),
    ]
    args = [sinks, slopes, y_in, y_in, y_in]
    if cached:
        in_specs += [
            pl.BlockSpec((None, 1, WINDOW, kw), lambda b, i: (layer, b, 0, 0)),
            pl.BlockSpec((None, 1, WINDOW, kw), lambda b, i: (layer, b, 0, 0)),
        ]
        args += [win_k, win_v]
    else:
        in_specs += [
            pl.BlockSpec((lq, kw), lambda b, i: (rb + b * nq + jnp.maximum(i - 1, 0), COL_K // kw)),
            pl.BlockSpec((lq, kw), lambda b, i: (rb + b * nq + jnp.maximum(i - 1, 0), COL_V // kw)),
        ]
        args += [y_in, y_in]
    in_specs += [pl.BlockSpec((1, LANES), lambda b, i: (0, 0)), pl.BlockSpec((1, LANES), lambda b, i: (0, 0))]
    args += [gq2, gk2]
    return pl.pallas_call(
        functools.partial(_swa_kernel, lq=lq, cached=cached),
        grid=(nb, nq),
        in_specs=in_specs,
        out_specs=[
            pl.BlockSpec((lq, D_MODEL), lambda b, i: (b * nq + i, 0)),
            pl.BlockSpec((lq, kw), lambda b, i: (b * nq + i, 0)),
        ],
        out_shape=[
            _mixer_out_shape(cached, nb * nq * lq),
            jax.ShapeDtypeStruct((nb * nq * lq, kw), F32),
        ],
        compiler_params=_cparams(("parallel", "arbitrary")),
        name="swa_lq%d" % lq,
    )(*args)


def _mem_attn_kernel(q_ref, k_ref, v_ref, o_ref, *, cached):
    scale = MEM_HEAD_DIM ** -0.5
    for h in range(MEM_HEADS):
        sl = slice(h * MEM_HEAD_DIM, (h + 1) * MEM_HEAD_DIM)
        qh = q_ref[:, sl].astype(BF16)
        if cached:
            kh = k_ref[0, :, sl].astype(BF16)
            vh = v_ref[0, :, sl].astype(BF16)
        else:
            kh = k_ref[:, sl].astype(BF16)
            vh = v_ref[:, sl].astype(BF16)
        s = _nt_dot(qh, kh) * scale
        m = jnp.max(s, axis=-1, keepdims=True)
        p = jnp.exp(s - m)
        den = jnp.sum(p, axis=-1, keepdims=True)
        o_ref[:, sl] = (jnp.dot(p.astype(BF16), vh, preferred_element_type=F32) / den).astype(o_ref.dtype)


def _mem_attn(y_in, mem_k, mem_v, cached, layer, *, nb, tq, nq, row0):
    rb = row0 // tq
    in_specs = [pl.BlockSpec((tq, D_MODEL), lambda b, i: (rb + b * nq + i, COL_QM // D_MODEL))]
    args = [y_in, mem_k, mem_v]
    if cached:
        in_specs += [
            pl.BlockSpec((None, 1, N_MEM, D_MODEL), lambda b, i: (layer, b, 0, 0)),
            pl.BlockSpec((None, 1, N_MEM, D_MODEL), lambda b, i: (layer, b, 0, 0)),
        ]
    else:
        in_specs += [
            pl.BlockSpec((N_MEM, D_MODEL), lambda b, i: (b, 0)),
            pl.BlockSpec((N_MEM, D_MODEL), lambda b, i: (b, 1)),
        ]
    return pl.pallas_call(
        functools.partial(_mem_attn_kernel, cached=cached),
        grid=(nb, nq),
        in_specs=in_specs,
        out_specs=pl.BlockSpec((tq, D_MODEL), lambda b, i: (b * nq + i, 0)),
        out_shape=_mixer_out_shape(cached, nb * nq * tq),
        compiler_params=_cparams(("parallel", "arbitrary")),
        name="mem_attn_tq%d" % tq,
    )(*args)


def _merge_kernel(a0_ref, a1_ref, a2_ref, w0_ref, w1_ref, w2_ref, g0_ref, g1_ref, g2_ref,
                  b0_ref, b1_ref, b2_ref, o_ref):
    acc = None
    for a_ref, w_ref, g_ref, b_ref in ((a0_ref, w0_ref, g0_ref, b0_ref), (a1_ref, w1_ref, g1_ref, b1_ref),
                                       (a2_ref, w2_ref, g2_ref, b2_ref)):
        br = jnp.dot(a_ref[...], w_ref[...], preferred_element_type=F32)
        t = jax.nn.sigmoid(g_ref[...] + b_ref[...]) * br
        acc = t if acc is None else acc + t
    o_ref[...] = acc.astype(BF16)


def _merge(a0, a1, a2, w0, w1, w2, y_in, b_gate, layer, *, tm, tn):
    gb = COL_GATE // tn
    nbk = D_MODEL // tn
    a_spec = pl.BlockSpec((tm, D_MODEL), lambda i, j: (i, 0))
    w_spec = pl.BlockSpec((None, D_MODEL, tn), lambda i, j: (layer, 0, j))
    return pl.pallas_call(
        _merge_kernel,
        grid=(T_ALL // tm, nbk),
        in_specs=[a_spec, a_spec, a_spec, w_spec, w_spec, w_spec,
                  pl.BlockSpec((tm, tn), lambda i, j: (i, gb + j)),
                  pl.BlockSpec((tm, tn), lambda i, j: (i, gb + nbk + j)),
                  pl.BlockSpec((tm, tn), lambda i, j: (i, gb + 2 * nbk + j)),
                  pl.BlockSpec((1, tn), lambda i, j: (0, j)),
                  pl.BlockSpec((1, tn), lambda i, j: (0, nbk + j)),
                  pl.BlockSpec((1, tn), lambda i, j: (0, 2 * nbk + j))],
        out_specs=pl.BlockSpec((tm, tn), lambda i, j: (i, j)),
        out_shape=jax.ShapeDtypeStruct((T_ALL, D_MODEL), BF16),
        compiler_params=_cparams(("parallel", "arbitrary")),
        name="gated_merge",
    )(a0, a1, a2, w0, w1, w2, y_in, y_in, y_in, b_gate, b_gate, b_gate)


def _matmul_res_kernel(a_ref, w_ref, r_ref, o_ref):
    o_ref[...] = r_ref[...] + jnp.dot(a_ref[...], w_ref[...], preferred_element_type=F32)


def _matmul_res(a, w, r, layer, *, tn):
    m, k = a.shape
    n = w.shape[2]
    return pl.pallas_call(
        _matmul_res_kernel,
        grid=(m // TM, n // tn),
        in_specs=[pl.BlockSpec((TM, k), lambda i, j: (i, 0)),
                  pl.BlockSpec((None, k, tn), lambda i, j: (layer, 0, j)),
                  pl.BlockSpec((TM, tn), lambda i, j: (i, j))],
        out_specs=pl.BlockSpec((TM, tn), lambda i, j: (i, j)),
        out_shape=jax.ShapeDtypeStruct((m, n), F32),
        compiler_params=_cparams(("parallel", "arbitrary")),
        name="matmul_residual",
    )(a, w, r)


def _top_values(s, k):
    vals = []
    cur = s
    for r in range(k):
        mx = jnp.max(cur, axis=0, keepdims=True)
        vals.append(mx)
        if r + 1 < k:
            cur = jnp.where(cur == mx, NEG_INF, cur)
    return vals


def _bitonic_pairs(n):
    pairs = []
    k = 2
    while k <= n:
        j = k // 2
        while j >= 1:
            pairs += [(i, i ^ j, (i & k) == 0) for i in range(n) if (i ^ j) > i]
            j //= 2
        k *= 2
    return pairs


def _top_values_sorted(s, k):
    groups = s.shape[0] // SUBLANES
    v = [s[a * SUBLANES:(a + 1) * SUBLANES, :] for a in range(groups)]
    for i, l, larger_first in _bitonic_pairs(groups):
        hi, lo = jnp.maximum(v[i], v[l]), jnp.minimum(v[i], v[l])
        v[i], v[l] = (hi, lo) if larger_first else (lo, hi)
    vals = []
    for r in range(k):
        mx = jnp.max(v[0], axis=0, keepdims=True)
        vals.append(mx)
        if r + 1 < k:
            hit = v[0] == mx
            for d in range(k - 1 - r):
                v[d] = jnp.where(hit, v[d + 1], v[d])
    return vals


def _router_kernel(x_ref, g_ref, wqt_ref, keys_ref, h2_ref, e0_ref, th_ref, e1_ref):
    h2 = _rms(x_ref[...], g_ref[...]).astype(BF16)
    h2_ref[...] = h2
    qt = _nt_dot(wqt_ref[...], h2)
    for h in range(PEER_HEADS):
        s0 = jnp.dot(keys_ref[2 * h], qt[(2 * h) * PEER_HALF:(2 * h + 1) * PEER_HALF, :].astype(BF16),
                     preferred_element_type=F32)
        s1 = jnp.dot(keys_ref[2 * h + 1], qt[(2 * h + 1) * PEER_HALF:(2 * h + 2) * PEER_HALF, :].astype(BF16),
                     preferred_element_type=F32)
        top0 = _top_values_sorted(s0, PEER_TOPK)
        top1 = _top_values_sorted(s1, PEER_TOPK)
        a1 = jnp.concatenate(top1, axis=0)
        half_k = PEER_TOPK // 2
        cand = jnp.concatenate(
            [top0[0] + a1]
            + [top0[r] + a1[:half_k] for r in range(1, half_k)]
            + [jnp.concatenate(top0[half_k:], axis=0) + top1[0]], axis=0)
        best = _top_values(cand, PEER_TOPK)
        m = best[0]
        tau = best[PEER_TOPK - 1]
        z = jnp.zeros_like(m)
        for r in range(PEER_TOPK):
            z = z + jnp.exp(best[r] - m)
        e1_sorted = jnp.exp(a1 - top1[0])
        th = jnp.full(s0.shape, NEVER, F32)
        for r in range(PEER_TOPK):
            th_r = jnp.min(jnp.where((top0[r] + a1) >= tau, e1_sorted, NEVER), axis=0, keepdims=True)
            th = jnp.where(s0 == top0[r], th_r, th)
        e0_ref[h] = 0.5 * jnp.exp(s0 - top0[0]) / z
        th_ref[h] = th
        e1_ref[h] = jnp.exp(s1 - top1[0])


def _router(x, g, wqt, keys, layer, *, tmr):
    t = x.shape[0]
    tok = pl.BlockSpec((PEER_HEADS, N_KEYS, tmr), lambda i: (0, 0, i))
    per_key = jax.ShapeDtypeStruct((PEER_HEADS, N_KEYS, t), F32)
    return pl.pallas_call(
        _router_kernel,
        grid=(t // tmr,),
        in_specs=[pl.BlockSpec((tmr, D_MODEL), lambda i: (i, 0)),
                  pl.BlockSpec((1, D_MODEL), lambda i: (0, 0)),
                  pl.BlockSpec((None, PEER_HEADS * PEER_QUERY_DIM, D_MODEL), lambda i: (layer, 0, 0)),
                  pl.BlockSpec((None, 2 * PEER_HEADS, N_KEYS, PEER_HALF), lambda i: (layer, 0, 0, 0))],
        out_specs=[pl.BlockSpec((tmr, D_MODEL), lambda i: (i, 0)), tok, tok, tok],
        out_shape=[jax.ShapeDtypeStruct((t, D_MODEL), BF16), per_key, per_key, per_key],
        compiler_params=_cparams(("parallel",)),
        name="peer_router",
    )(x, g, wqt, keys)


def _peer_dense_kernel(h2_ref, u_ref, vt_ref, e0_ref, th_ref, e1_ref, x_ref, o_ref,
                       acc_ref, at_ref, wa_ref, wb_ref, h2t_ref, *, te, ne, tm):
    e = pl.program_id(1)
    per_step = te // N_KEYS
    assert 2 * per_step == SUBLANES

    @pl.when(e == 0)
    def _():
        acc_ref[...] = jnp.zeros_like(acc_ref)
        wb_ref[...] = jnp.zeros_like(wb_ref)
        h2t_ref[...] = h2_ref[...].T

    chunk = 2 * LANES
    n_chunks = tm // chunk
    vrows = D_MODEL // (2 * per_step)

    def value_piece(w_old_ref, c, m):
        rows = slice(m * vrows, (m + 1) * vrows)
        cols = slice(c * chunk, (c + 1) * chunk)
        acc_ref[rows, cols] += jnp.dot(vt_ref[rows, :], w_old_ref[:, cols], preferred_element_type=F32)

    def step(w_new_ref, w_old_ref, parity):
        def score_piece(c, ii):
            rows = slice(ii * N_KEYS, (ii + 1) * N_KEYS)
            cols = slice(c * chunk, (c + 1) * chunk)
            at_ref[rows, cols] = jnp.dot(u_ref[rows, :], h2t_ref[:, cols], preferred_element_type=F32)

        def gate_cell(ii, tg):
            rows = slice(ii * N_KEYS, (ii + 1) * N_KEYS)
            cs = slice(tg * LANES, (tg + 1) * LANES)
            r = parity * per_step + ii
            gate = None
            for h in range(PEER_HEADS):
                e1 = e1_ref[h, :, cs]
                t = jnp.where(e1 >= th_ref[h, r:r + 1, cs], e1 * e0_ref[h, r:r + 1, cs], 0.0)
                gate = t if gate is None else gate + t
            a = at_ref[rows, cs]
            gelu2 = a * (1.0 + lax.erf(a * (2.0 ** -0.5)))
            w_new_ref[rows, cs] = (gate * gelu2).astype(BF16)

        rounds = [(c, ii) for c in range(n_chunks) for ii in range(per_step)]
        score_piece(*rounds[0])
        for k, (c, ii) in enumerate(rounds):
            if k + 1 < len(rounds):
                score_piece(*rounds[k + 1])
            gate_cell(ii, 2 * c)
            value_piece(w_old_ref, c, 2 * ii)
            gate_cell(ii, 2 * c + 1)
            value_piece(w_old_ref, c, 2 * ii + 1)

    @pl.when(jnp.logical_and(e % 2 == 0, e < ne))
    def _():
        step(wa_ref, wb_ref, 0)

    @pl.when(jnp.logical_and(e % 2 == 1, e < ne))
    def _():
        step(wb_ref, wa_ref, 1)

    @pl.when(e == ne)
    def _():
        w_last_ref = wb_ref if ne % 2 == 0 else wa_ref
        for c in range(n_chunks):
            for m in range(D_MODEL // vrows):
                value_piece(w_last_ref, c, m)
        o_ref[...] = x_ref[...] + acc_ref[...].T


def _peer_dense(h2, u, vt, e0, th, e1, x, layer, *, te, tm):
    t = x.shape[0]
    ne = N_EXPERTS // te
    once = pl.Buffered(1)
    by_first = pl.BlockSpec((PEER_HEADS, SUBLANES, tm), lambda i, e: (0, jnp.minimum(e, ne - 1) // 2, i))
    by_second = pl.BlockSpec((PEER_HEADS, N_KEYS, tm), lambda i, e: (0, 0, i), pipeline_mode=once)
    return pl.pallas_call(
        functools.partial(_peer_dense_kernel, te=te, ne=ne, tm=tm),
        grid=(t // tm, ne + 1),
        in_specs=[pl.BlockSpec((tm, D_MODEL), lambda i, e: (i, 0), pipeline_mode=once),
                  pl.BlockSpec((None, te, D_MODEL), lambda i, e: (layer, jnp.minimum(e, ne - 1), 0)),
                  pl.BlockSpec((None, D_MODEL, te), lambda i, e: (layer, 0, jnp.maximum(e - 1, 0))),
                  by_first, by_first, by_second,
                  pl.BlockSpec((tm, D_MODEL), lambda i, e: (i, 0), pipeline_mode=once)],
        out_specs=pl.BlockSpec((tm, D_MODEL), lambda i, e: (i, 0)),
        out_shape=jax.ShapeDtypeStruct((t, D_MODEL), F32),
        scratch_shapes=[pltpu.VMEM((D_MODEL, tm), F32), pltpu.VMEM((te, tm), F32),
                        pltpu.VMEM((te, tm), BF16), pltpu.VMEM((te, tm), BF16),
                        pltpu.VMEM((D_MODEL, tm), BF16)],
        compiler_params=_cparams(("parallel", "arbitrary")),
        name="peer_dense",
    )(h2, u, vt, e0, th, e1, x)


def _join_groups(prompt_buf, sample_rows):
    return lax.dynamic_update_slice(prompt_buf, sample_rows.astype(BF16), (T_PROMPT, 0))


def _pad_lanes(v, n=LANES):
    return jnp.pad(v, (0, n - v.shape[0])).reshape(1, n)


def kernel(x_prompt, x_sample, cache_win_k, cache_win_v, state_ssm, state_conv, cache_mem_k, cache_mem_v,
           mem_prompt, w_in, b_gate, conv_w, conv_b, dt_bias, a_log, d_skip, g_ssd_norm, g_q, g_k,
           attn_sinks, g_mem, w_mem_kv, g_qm, g_km, w_o_ssm, w_o_swa, w_o_mem, w_out, g_mix, g_ffn,
           w_peer_q, peer_sub_keys, peer_u, peer_v):
    x = jnp.concatenate([x_prompt.reshape(T_PROMPT, D_MODEL), x_sample.reshape(T_SAMPLE, D_MODEL)], axis=0)
    slopes = jnp.exp2(-8.0 * jnp.arange(1, SWA_HEADS + 1, dtype=F32) / SWA_HEADS)
    mem_x = mem_prompt.reshape(BATCH * N_MEM, D_MODEL)
    kw = SWA_KV_HEADS * SWA_HEAD_DIM
    half = SSD_HEADS // 2
    row = lambda v: v.reshape(1, -1)

    w_in_t = jnp.swapaxes(w_in, 1, 2)
    w_mem_kv_b = w_mem_kv.astype(BF16)
    w_o_ssm_b, w_o_swa_b, w_o_mem_b = w_o_ssm.astype(BF16), w_o_swa.astype(BF16), w_o_mem.astype(BF16)
    w_out_b = w_out.astype(BF16)
    wqt_b = jnp.swapaxes(w_peer_q, 1, 2).astype(BF16)
    keys_b = peer_sub_keys.reshape(DEPTH, 2 * PEER_HEADS, N_KEYS, PEER_HALF).astype(BF16)
    u_b = peer_u.astype(BF16)
    vt_b = jnp.swapaxes(peer_v, 1, 2).astype(BF16)
    conv0 = jnp.pad(state_conv, ((0, 0), (0, 0), (SUBLANES - (CONV_W - 1), 0), (0, 0)))
    ssm0 = state_ssm.reshape(DEPTH, DEC_BATCH, half, LANES, D_STATE)
    wk_c = cache_win_k.reshape(DEPTH, DEC_BATCH, WINDOW, kw)
    wv_c = cache_win_v.reshape(DEPTH, DEC_BATCH, WINDOW, kw)
    mk_c = cache_mem_k.reshape(DEPTH, DEC_BATCH, N_MEM, D_MODEL)
    mv_c = cache_mem_v.reshape(DEPTH, DEC_BATCH, N_MEM, D_MODEL)

    outs = {k: [] for k in ("p_wk", "p_wv", "p_ssm", "p_conv", "p_mk", "p_mv", "s_wk", "s_wv", "s_ssm", "s_conv")}
    for l in range(DEPTH):
        mem_kv = _norm_matmul(mem_x, row(g_mem[l]), w_mem_kv_b, row(g_km[l]), l,
                              tm=512, tn=MEM_HEAD_DIM, norm_lo=0, norm_hi=MEM_HEADS)
        outs["p_mk"].append(mem_kv[:, :D_MODEL].reshape(BATCH, N_MEM, MEM_HEADS, MEM_HEAD_DIM))
        outs["p_mv"].append(mem_kv[:, D_MODEL:].reshape(BATCH, N_MEM, MEM_HEADS, MEM_HEAD_DIM))

        y = _in_proj(_rms_cast(x, row(g_mix[l])), w_in_t, row(g_qm[l]), l)

        cw, cb = conv_w[l], row(conv_b[l])
        dtb, alog = _pad_lanes(dt_bias[l]), _pad_lanes(a_log[l])
        dskip = row(jnp.repeat(d_skip[l], SSD_HEAD_DIM))
        gn = row(g_ssd_norm[l])
        y_ssm, p_ssm, p_tail = _ssd(y, cw, cb, dtb, alog, dskip, gn, None, None, l,
                                    nb=BATCH, q=SSD_CHUNK, nc=SEQ // SSD_CHUNK, row0=0)
        y_ssm_s, s_ssm, s_tail = _ssd(y, cw, cb, dtb, alog, dskip, gn, conv0, ssm0, l,
                                      nb=DEC_BATCH, q=DEC_SEQ, nc=1, row0=T_PROMPT)
        y_ssm = _join_groups(y_ssm, y_ssm_s)
        outs["p_ssm"].append(p_ssm.reshape(BATCH, SSD_HEADS, SSD_HEAD_DIM, D_STATE))
        outs["s_ssm"].append(s_ssm.reshape(DEC_BATCH, SSD_HEADS, SSD_HEAD_DIM, D_STATE))
        outs["p_conv"].append(p_tail[:, SUBLANES - (CONV_W - 1):])
        outs["s_conv"].append(s_tail[:, SUBLANES - (CONV_W - 1):])

        gq2 = row(jnp.concatenate([g_q[l], g_q[l]]))
        gk2 = row(jnp.concatenate([g_k[l], g_k[l]]))
        o_swa, kn_p = _swa(y, attn_sinks[l], slopes, gq2, gk2, None, None, l,
                           nb=BATCH, lq=WINDOW, nq=SEQ // WINDOW, row0=0)
        o_swa_s, kn_s = _swa(y, attn_sinks[l], slopes, gq2, gk2, wk_c, wv_c, l,
                             nb=DEC_BATCH, lq=DEC_SEQ, nq=1, row0=T_PROMPT)
        o_swa = _join_groups(o_swa, o_swa_s)
        v_p = y[:T_PROMPT, COL_V:COL_V + kw].reshape(BATCH, SEQ, kw)
        v_s = y[T_PROMPT:, COL_V:COL_V + kw].reshape(DEC_BATCH, DEC_SEQ, kw)
        kv_shape = (SWA_KV_HEADS, SWA_HEAD_DIM)
        outs["p_wk"].append(kn_p.reshape(BATCH, SEQ, kw)[:, SEQ - WINDOW:].reshape(BATCH, WINDOW, *kv_shape))
        outs["p_wv"].append(v_p[:, SEQ - WINDOW:].reshape(BATCH, WINDOW, *kv_shape))
        outs["s_wk"].append(jnp.concatenate([wk_c[l][:, DEC_SEQ:], kn_s.reshape(DEC_BATCH, DEC_SEQ, kw)], axis=1)
                            .reshape(DEC_BATCH, WINDOW, *kv_shape))
        outs["s_wv"].append(jnp.concatenate([wv_c[l][:, DEC_SEQ:], v_s], axis=1)
                            .reshape(DEC_BATCH, WINDOW, *kv_shape))

        o_mem = _mem_attn(y, mem_kv, mem_kv, False, l, nb=BATCH, tq=512, nq=SEQ // 512, row0=0)
        o_mem_s = _mem_attn(y, mk_c, mv_c, True, l, nb=DEC_BATCH, tq=DEC_SEQ, nq=1, row0=T_PROMPT)
        o_mem = _join_groups(o_mem, o_mem_s)

        merged = _merge(y_ssm, o_swa, o_mem, w_o_ssm_b, w_o_swa_b, w_o_mem_b, y, row(b_gate[l]), l,
                        tm=TM, tn=512)
        x = _matmul_res(merged, w_out_b, x, l, tn=512)

        h2, e0, th, e1 = _router(x, row(g_ffn[l]), wqt_b, keys_b, l, tmr=256)
        x = _peer_dense(h2, u_b, vt_b, e0, th, e1, x, l, te=512, tm=TM)

    yp = x[:T_PROMPT].reshape(BATCH, SEQ, D_MODEL)
    ys = x[T_PROMPT:].reshape(DEC_BATCH, DEC_SEQ, D_MODEL)
    st = lambda k: jnp.stack(outs[k])
    return (yp, ys, st("p_wk"), st("p_wv"), st("p_ssm"), st("p_conv"), st("p_mk"), st("p_mv"),
            st("s_wk"), st("s_wv"), st("s_ssm"), st("s_conv"))
```
